```python
import math
import functools
import jax
import jax.numpy as jnp
from jax import lax
import numpy as np

D_MODEL = 2048
BATCH = 4
SEQ = 2048
DEPTH = 4
DEC_BATCH = 8
DEC_SEQ = 8
PAST_LEN = 16384
PAGE_SIZE = 128

N_EVEN = (DEPTH + 1) // 2
N_ODD = DEPTH // 2

HEAD_DIM = 128
ATT_HEADS = D_MODEL // HEAD_DIM
ATT_KV_HEADS = ATT_HEADS // 4
IDX_HEADS = 16
IDX_DIM = 64
TOPK_MAX = 256
Q_BLOCK = 128
SSM_HEAD_DIM = 64
SSM_D = D_MODEL
SSM_HEADS = SSM_D // SSM_HEAD_DIM
SSM_GROUPS = 4
SSM_STATE = 128
SSM_CONV = 4
SSM_CHUNK = 128
SSM_CONV_DIM = SSM_D + 2 * SSM_GROUPS * SSM_STATE
SC_DIM = D_MODEL
SC_WIDTH = 3
GM_DIM = D_MODEL
GM_GROUPS = 16
GM_CHUNK = 128
N_EXPERTS = 32
TOP_K = 4
D_EXPERT = D_MODEL
SWIGLU_LIMIT = 7.0
SWIGLU_ALPHA = 1.702
MOE_BLOCK = 128
DN_ALPHA = (2 * DEPTH) ** 0.25
DN_BETA = (8 * DEPTH) ** -0.25
LN_EPS = 1e-5
RMS_EPS = 1e-5

A_Q = ATT_HEADS * HEAD_DIM
A_KV = ATT_KV_HEADS * HEAD_DIM
EVEN_SIZES = (A_Q, A_KV, A_KV, IDX_HEADS * IDX_DIM, IDX_DIM, IDX_HEADS, SSM_D, SSM_CONV_DIM, SSM_HEADS)
EVEN_IN = sum(EVEN_SIZES)
EVEN_OUT = A_Q + SSM_D
ODD_SIZES = (SC_DIM, SC_DIM, SC_DIM, 2 * GM_DIM)
ODD_IN = sum(ODD_SIZES)
ODD_OUT = SC_DIM + GM_DIM

kernel_name = "hybrid_dsa_ssd_shortconv_gmlp_moe_step"


def split_cols(t, sizes):
    return jnp.split(t, np.cumsum(sizes)[:-1].tolist(), axis=-1)


def layer_norm(x, g, b):
    xf = x.astype(jnp.float32)
    mu = jnp.mean(xf, axis=-1, keepdims=True)
    var = jnp.mean(jnp.square(xf - mu), axis=-1, keepdims=True)
    return ((xf - mu) * lax.rsqrt(var + LN_EPS) * g + b).astype(x.dtype)


def adaln_mod(c, w, b):
    m = jax.nn.silu(c) @ w + b
    return jnp.split(m[:, None, :], 6, axis=-1)


def modulate(x, shift, scale):
    return x * (1 + scale) + shift


def residual_norm(x, f, gate, g, b):
    return layer_norm(DN_ALPHA * x + (1 + gate) * f, g, b)


def causal_dwconv(x, prev, w):
    width = w.shape[0]
    L = x.shape[1]
    xp = jnp.concatenate([prev.astype(x.dtype), x], axis=1)
    y = sum(xp[:, j:j + L] * w[j] for j in range(width))
    return y, xp[:, L:]


def index_scores(qi, wi, ki):
    dots = jnp.einsum('bqhd,bsd->bqhs', qi.astype(jnp.float32), ki.astype(jnp.float32)) * IDX_DIM ** -0.5
    return jnp.einsum('bqhs,bqh->bqs', jax.nn.relu(dots), wi.astype(jnp.float32) * IDX_HEADS ** -0.5)


def sparse_attend(q, k_sel, v_sel, valid):
    b, nq = q.shape[:2]
    qg = q.reshape(b, nq, ATT_KV_HEADS, ATT_HEADS // ATT_KV_HEADS, HEAD_DIM)
    logits = jnp.einsum('bqgrd,bqngd->bqgrn', qg, k_sel, preferred_element_type=jnp.float32) * HEAD_DIM ** -0.5
    logits = jnp.where(valid[:, :, None, None, :], logits, -jnp.inf)
    p = jax.nn.softmax(logits, axis=-1).astype(v_sel.dtype)
    out = jnp.einsum('bqgrn,bqngd->bqgrd', p, v_sel)
    return out.reshape(b, nq, A_Q)


def dsa_prompt(q, k, v, qi, ki, wi):
    b, L = q.shape[:2]
    n_sel = min(TOPK_MAX, L // 4)
    key_pos = jnp.arange(L)
    take = jax.vmap(lambda rows, idx: rows[idx])

    def block(start):
        sl = lambda t: lax.dynamic_slice_in_dim(t, start, Q_BLOCK, axis=1)
        pos = start + jnp.arange(Q_BLOCK)
        scores = index_scores(sl(qi), sl(wi), ki)
        scores = jnp.where(key_pos[None, None, :] <= pos[None, :, None], scores, -jnp.inf)
        _, idx = lax.top_k(scores, n_sel)
        return sparse_attend(sl(q), take(k, idx), take(v, idx), idx <= pos[None, :, None])

    out = lax.map(block, jnp.arange(0, L, Q_BLOCK))
    return jnp.moveaxis(out, 0, 1).reshape(b, L, A_Q)


def gather_rows(pool, new_rows, page_table, idx):
    past = page_table.shape[1] * PAGE_SIZE
    bidx = jnp.arange(idx.shape[0])[:, None, None]
    s_past = jnp.minimum(idx, past - 1)
    from_past = pool[page_table[bidx, s_past // PAGE_SIZE], s_past % PAGE_SIZE]
    from_new = new_rows[bidx, jnp.clip(idx - past, 0, new_rows.shape[1] - 1)]
    is_new = (idx >= past)[..., None, None]
    return jnp.where(is_new, from_new, from_past.astype(new_rows.dtype))


def dsa_sample(q, k, v, qi, ki, wi, pool_k, pool_v, pool_ik, page_table):
    b, L_new = q.shape[:2]
    past = page_table.shape[1] * PAGE_SIZE
    n_keys = past + L_new
    n_sel = min(TOPK_MAX, n_keys // 4)
    ik_past = pool_ik[page_table].reshape(b, past, IDX_DIM)
    ik_all = jnp.concatenate([ik_past.astype(ki.dtype), ki], axis=1)
    pos = past + jnp.arange(L_new)
    scores = index_scores(qi, wi, ik_all)
    scores = jnp.where(jnp.arange(n_keys)[None, None, :] <= pos[None, :, None], scores, -jnp.inf)
    _, idx = lax.top_k(scores, n_sel)
    k_sel = gather_rows(pool_k, k, page_table, idx)
    v_sel = gather_rows(pool_v, v, page_table, idx)
    return sparse_attend(q, k_sel, v_sel, idx <= pos[None, :, None])


def ssd_scan(x, dt, a, bm, cm, h0, chunk):
    b, L, H, P = x.shape
    nc = L // chunk
    x = x.reshape(b, nc, chunk, H, P)
    dt = dt.reshape(b, nc, chunk, H)
    bm = bm.reshape(b, nc, chunk, H, -1)
    cm = cm.reshape(b, nc, chunk, H, -1)
    acum = jnp.cumsum(dt * a, axis=2)
    causal = jnp.tril(jnp.ones((chunk, chunk), bool))
    seg = acum[:, :, :, None, :] - acum[:, :, None, :, :]
    decay = jnp.exp(jnp.where(causal[:, :, None], seg, -jnp.inf))
    xdt = x * dt[..., None]
    scores = jnp.einsum('bclhn,bcshn->bclsh', cm, bm) * decay
    y_diag = jnp.einsum('bclsh,bcshp->bclhp', scores, xdt)
    decay_end = jnp.exp(acum[:, :, -1:, :] - acum)
    states = jnp.einsum('bcshn,bcsh,bcshp->bchpn', bm, decay_end, xdt)
    chunk_decay = jnp.exp(acum[:, :, -1, :])

    def step(h, inp):
        st, dec = inp
        return h * dec[:, :, None, None] + st, h

    h_last, h_in = lax.scan(step, h0, (jnp.moveaxis(states, 1, 0), jnp.moveaxis(chunk_decay, 1, 0)))
    h_in = jnp.moveaxis(h_in, 0, 1)
    y_off = jnp.einsum('bclhn,bchpn,bclh->bclhp', cm, h_in, jnp.exp(acum))
    return (y_diag + y_off).reshape(b, L, H, P), h_last


def mamba_branch(z, xbc, dt_raw, conv_prev, h0, conv_w, conv_b, dt_bias, a_log, d_skip, norm_w):
    b, L, _ = xbc.shape
    xbc, conv_state = causal_dwconv(xbc, conv_prev, conv_w)
    xbc = jax.nn.silu(xbc + conv_b)
    xs, bm, cm = jnp.split(xbc, [SSM_D, SSM_D + SSM_GROUPS * SSM_STATE], axis=-1)
    rep = SSM_HEADS // SSM_GROUPS
    xs = xs.reshape(b, L, SSM_HEADS, SSM_HEAD_DIM).astype(jnp.float32)
    bm = jnp.repeat(bm.reshape(b, L, SSM_GROUPS, SSM_STATE), rep, axis=2).astype(jnp.float32)
    cm = jnp.repeat(cm.reshape(b, L, SSM_GROUPS, SSM_STATE), rep, axis=2).astype(jnp.float32)
    dt = jax.nn.softplus(dt_raw.astype(jnp.float32) + dt_bias.astype(jnp.float32))
    a = -jnp.exp(a_log.astype(jnp.float32))
    chunk = SSM_CHUNK if L % SSM_CHUNK == 0 else L
    y, h_last = ssd_scan(xs, dt, a, bm, cm, h0.astype(jnp.float32), chunk)
    y = y + d_skip.astype(jnp.float32)[:, None] * xs
    y = y.reshape(b, L, SSM_D) * jax.nn.silu(z.astype(jnp.float32))
    yg = y.reshape(b, L, SSM_GROUPS, SSM_D // SSM_GROUPS)
    yg = yg * lax.rsqrt(jnp.mean(jnp.square(yg), axis=-1, keepdims=True) + RMS_EPS)
    y = yg.reshape(b, L, SSM_D) * norm_w
    return y.astype(z.dtype), h_last, conv_state


def even_mixer(h, attend, conv_prev, h0, w_in, w_out, mamba_w):
    b, L, _ = h.shape
    q, k, v, qi, ki, wi, z, xbc, dt = split_cols(h @ w_in, EVEN_SIZES)
    q = q.reshape(b, L, ATT_HEADS, HEAD_DIM)
    k = k.reshape(b, L, ATT_KV_HEADS, HEAD_DIM)
    v = v.reshape(b, L, ATT_KV_HEADS, HEAD_DIM)
    qi = qi.reshape(b, L, IDX_HEADS, IDX_DIM)
    att = attend(q, k, v, qi, ki, wi)
    ssm, h_last, conv_state = mamba_branch(z, xbc, dt, conv_prev, h0, *mamba_w)
    y = jnp.concatenate([att, ssm], axis=-1) @ w_out
    return y, (k, v, ki, h_last, conv_state)


def gmlp_branch(uv, ln_g, ln_b, ws, bs):
    b, L, _ = uv.shape
    u, v = jnp.split(jax.nn.gelu(uv, approximate=False), 2, axis=-1)
    v = layer_norm(v, ln_g, ln_b)
    cs = min(GM_CHUNK, L)
    nc = L // cs
    gd = GM_DIM // GM_GROUPS
    mask = jnp.tril(jnp.ones((cs, cs), bool))
    wm = jnp.where(mask, ws[:, :cs, :cs], 0.0).astype(v.dtype)
    vc = v.reshape(b, nc, cs, GM_GROUPS, gd)
    mixed = jnp.einsum('gts,bnsgc->bntgc', wm, vc) + bs[:, :cs].T[None, None, :, :, None]
    return u * mixed.reshape(b, L, GM_DIM), v[:, L - cs:]


def odd_mixer(h, sconv_prev, w_in, w_out, sconv_w, gm_ln_g, gm_ln_b, gm_ws, gm_bs):
    bg, cg, sv, uv = split_cols(h @ w_in, ODD_SIZES)
    sc, sc_state = causal_dwconv(cg * sv, sconv_prev, sconv_w)
    sc = bg * sc
    gm, gm_v = gmlp_branch(uv, gm_ln_g, gm_ln_b, gm_ws, gm_bs)
    y = jnp.concatenate([sc, gm], axis=-1) @ w_out
    return y, (sc_state, gm_v)


def expert_swiglu(x, w_gu, b_gu, w_down, b_down):
    gu = x @ w_gu + b_gu
    g, u = jnp.split(gu, 2, axis=-1)
    g = jnp.minimum(g, SWIGLU_LIMIT)
    u = jnp.clip(u, -SWIGLU_LIMIT, SWIGLU_LIMIT)
    return ((u + 1) * (g * jax.nn.sigmoid(SWIGLU_ALPHA * g))) @ w_down + b_down


def moe_ffn(h, router_w, router_b, w_gu, b_gu, w_down, b_down):
    n, d = h.shape
    nk = n * TOP_K
    bm = max(8, min(MOE_BLOCK, nk // N_EXPERTS))
    n_blocks = -(-nk // bm) + N_EXPERTS
    logits = h.astype(jnp.float32) @ router_w.astype(jnp.float32) + router_b.astype(jnp.float32)
    top_logit, top_e = lax.top_k(logits, TOP_K)
    gate = jax.nn.softmax(top_logit, axis=-1)
    flat_e = top_e.reshape(-1)
    order = jnp.argsort(flat_e)
    sorted_e = flat_e[order]
    token_of = order // TOP_K
    counts = jnp.bincount(flat_e, length=N_EXPERTS)
    blocks_per_e = (counts + bm - 1) // bm
    block_end = jnp.cumsum(blocks_per_e)
    block_start = block_end - blocks_per_e
    rank = jnp.arange(nk) - (jnp.cumsum(counts) - counts)[sorted_e]
    dest = block_start[sorted_e] * bm + rank
    row_src = jnp.full((n_blocks * bm,), n, jnp.int32).at[dest].set(token_of)
    h_pad = jnp.concatenate([h, jnp.zeros((1, d), h.dtype)], axis=0)
    xb = h_pad[row_src].reshape(n_blocks, bm, d)
    blk = jnp.arange(n_blocks)
    block_e = jnp.minimum(jnp.searchsorted(block_end, blk, side='right'), N_EXPERTS - 1)
    live = blk < block_end[-1]

    def run_block(args):
        xblk, e, is_live = args
        return lax.cond(is_live,
                        lambda: expert_swiglu(xblk, w_gu[e], b_gu[e], w_down[e], b_down[e]),
                        lambda: jnp.zeros_like(xblk))

    yb = lax.map(run_block, (xb, block_e, live)).reshape(-1, d)
    y_assign = yb[dest] * gate.reshape(-1)[order][:, None].astype(yb.dtype)
    return jax.ops.segment_sum(y_assign, token_of, num_segments=n)


def channel_sublayer(x, m, g, b, moe_w):
    h = modulate(x, m[3], m[4])
    f = moe_ffn(h.reshape(-1, D_MODEL), *moe_w).reshape(x.shape)
    return residual_norm(x, f, m[5], g, b)


def setup_inputs(seed: int = 0) -> dict:
    key = jax.random.key(seed)
    ks = iter(jax.random.split(key, 48))

    def nrm(shape, scale):
        return jax.random.normal(next(ks), shape, jnp.float32) * scale

    n_pages = PAST_LEN // PAGE_SIZE
    n_used = DEC_BATCH * n_pages
    n_pool = n_used + n_used // 4
    page_table = jax.random.permutation(next(ks), n_pool)[:n_used].reshape(DEC_BATCH, n_pages).astype(jnp.int32)
    v_lo = A_Q + A_KV
    even_w_in = nrm((N_EVEN, D_MODEL, EVEN_IN), D_MODEL ** -0.5)
    even_w_in = even_w_in.at[:, :, v_lo:v_lo + A_KV].multiply(DN_BETA)
    dt = jnp.exp(jax.random.uniform(next(ks), (N_EVEN, SSM_HEADS)) * (math.log(0.1) - math.log(0.001)) + math.log(0.001))
    return {
        'x_prompt': nrm((BATCH, SEQ, D_MODEL), 1.0),
        'x_sample': nrm((DEC_BATCH, DEC_SEQ, D_MODEL), 1.0),
        'cache_k': nrm((N_EVEN, n_pool, PAGE_SIZE, ATT_KV_HEADS, HEAD_DIM), 1.0),
        'cache_v': nrm((N_EVEN, n_pool, PAGE_SIZE, ATT_KV_HEADS, HEAD_DIM), DN_BETA),
        'cache_idx_k': nrm((N_EVEN, n_pool, PAGE_SIZE, IDX_DIM), 1.0),
        'state_ssm': nrm((N_EVEN, DEC_BATCH, SSM_HEADS, SSM_HEAD_DIM, SSM_STATE), 0.1),
        'state_ssm_conv': nrm((N_EVEN, DEC_BATCH, SSM_CONV - 1, SSM_CONV_DIM), 1.0),
        'state_sconv': nrm((N_ODD, DEC_BATCH, SC_WIDTH - 1, SC_DIM), 0.5),
        'page_table': page_table,
        'c_prompt': nrm((BATCH, D_MODEL), 1.0),
        'c_sample': nrm((DEC_BATCH, D_MODEL), 1.0),
        'ada_w': nrm((DEPTH, D_MODEL, 6 * D_MODEL), 0.2 * D_MODEL ** -0.5),
        'ada_b': nrm((DEPTH, 6 * D_MODEL), 0.01),
        'ln_g': 1.0 + nrm((DEPTH, 2, D_MODEL), 0.1),
        'ln_b': nrm((DEPTH, 2, D_MODEL), 0.01),
        'even_w_in': even_w_in,
        'even_w_out': nrm((N_EVEN, EVEN_OUT, D_MODEL), DN_BETA * EVEN_OUT ** -0.5),
        'ssm_conv_w': nrm((N_EVEN, SSM_CONV, SSM_CONV_DIM), SSM_CONV ** -0.5),
        'ssm_conv_b': nrm((N_EVEN, SSM_CONV_DIM), 0.01),
        'ssm_dt_bias': dt + jnp.log(-jnp.expm1(-dt)),
        'ssm_a_log': jnp.log(jax.random.uniform(next(ks), (N_EVEN, SSM_HEADS), minval=1.0, maxval=16.0)),
        'ssm_d': 1.0 + nrm((N_EVEN, SSM_HEADS), 0.1),
        'ssm_norm_w': 1.0 + nrm((N_EVEN, SSM_D), 0.1),
        'odd_w_in': nrm((N_ODD, D_MODEL, ODD_IN), D_MODEL ** -0.5),
        'odd_w_out': nrm((N_ODD, ODD_OUT, D_MODEL), DN_BETA * ODD_OUT ** -0.5),
        'sconv_w': nrm((N_ODD, SC_WIDTH, SC_DIM), SC_WIDTH ** -0.5),
        'gmlp_ln_g': 1.0 + nrm((N_ODD, GM_DIM), 0.1),
        'gmlp_ln_b': nrm((N_ODD, GM_DIM), 0.01),
        'gmlp_ws': nrm((N_ODD, GM_GROUPS, GM_CHUNK, GM_CHUNK), GM_CHUNK ** -0.5),
        'gmlp_bs': 1.0 + nrm((N_ODD, GM_GROUPS, GM_CHUNK), 0.1),
        'router_w': nrm((DEPTH, D_MODEL, N_EXPERTS), D_MODEL ** -0.5),
        'router_b': nrm((DEPTH, N_EXPERTS), 0.01),
        'moe_w_gu': nrm((DEPTH, N_EXPERTS, D_MODEL, 2 * D_EXPERT), D_MODEL ** -0.5),
        'moe_b_gu': nrm((DEPTH, N_EXPERTS, 2 * D_EXPERT), 0.01),
        'moe_w_down': nrm((DEPTH, N_EXPERTS, D_EXPERT, D_MODEL), DN_BETA * D_EXPERT ** -0.5),
        'moe_b_down': nrm((DEPTH, N_EXPERTS, D_MODEL), 0.01),
    }


def reference(x_prompt, x_sample, cache_k, cache_v, cache_idx_k, state_ssm, state_ssm_conv, state_sconv,
              page_table, c_prompt, c_sample, ada_w, ada_b, ln_g, ln_b, even_w_in, even_w_out,
              ssm_conv_w, ssm_conv_b, ssm_dt_bias, ssm_a_log, ssm_d, ssm_norm_w, odd_w_in, odd_w_out,
              sconv_w, gmlp_ln_g, gmlp_ln_b, gmlp_ws, gmlp_bs, router_w, router_b,
              moe_w_gu, moe_b_gu, moe_w_down, moe_b_down):
    xp, xs = x_prompt, x_sample
    bp = xp.shape[0]
    kp, vp, ikp, ks_, vs_, iks = [], [], [], [], [], []
    ssm_p, ssm_s, sconvp, sconvs = [], [], [], []
    scp, scs, gvp, gvs = [], [], [], []
    for layer in range(DEPTH):
        mp = adaln_mod(c_prompt, ada_w[layer], ada_b[layer])
        ms = adaln_mod(c_sample, ada_w[layer], ada_b[layer])
        j = layer // 2
        if layer % 2 == 0:
            mamba_w = (ssm_conv_w[j], ssm_conv_b[j], ssm_dt_bias[j], ssm_a_log[j], ssm_d[j], ssm_norm_w[j])
            fp, (k_, v_, ik_, h_, c_) = even_mixer(
                modulate(xp, mp[0], mp[1]), dsa_prompt,
                jnp.zeros((bp, SSM_CONV - 1, SSM_CONV_DIM), xp.dtype),
                jnp.zeros((bp, SSM_HEADS, SSM_HEAD_DIM, SSM_STATE), jnp.float32),
                even_w_in[j], even_w_out[j], mamba_w)
            kp.append(k_); vp.append(v_); ikp.append(ik_); ssm_p.append(h_); sconvp.append(c_)
            attend_s = functools.partial(dsa_sample, pool_k=cache_k[j], pool_v=cache_v[j],
                                         pool_ik=cache_idx_k[j], page_table=page_table)
            fs, (k_, v_, ik_, h_, c_) = even_mixer(
                modulate(xs, ms[0], ms[1]), attend_s, state_ssm_conv[j], state_ssm[j],
                even_w_in[j], even_w_out[j], mamba_w)
            ks_.append(k_); vs_.append(v_); iks.append(ik_); ssm_s.append(h_); sconvs.append(c_)
        else:
            odd_w = (odd_w_in[j], odd_w_out[j], sconv_w[j], gmlp_ln_g[j], gmlp_ln_b[j], gmlp_ws[j], gmlp_bs[j])
            fp, (sc_, gv_) = odd_mixer(modulate(xp, mp[0], mp[1]),
                                       jnp.zeros((bp, SC_WIDTH - 1, SC_DIM), xp.dtype), *odd_w)
            scp.append(sc_); gvp.append(gv_)
            fs, (sc_, gv_) = odd_mixer(modulate(xs, ms[0], ms[1]), state_sconv[j], *odd_w)
            scs.append(sc_); gvs.append(gv_)
        xp = residual_norm(xp, fp, mp[2], ln_g[layer, 0], ln_b[layer, 0])
        xs = residual_norm(xs, fs, ms[2], ln_g[layer, 0], ln_b[layer, 0])
        moe_w = (router_w[layer], router_b[layer], moe_w_gu[layer], moe_b_gu[layer],
                 moe_w_down[layer], moe_b_down[layer])
        xp = channel_sublayer(xp, mp, ln_g[layer, 1], ln_b[layer, 1], moe_w)
        xs = channel_sublayer(xs, ms, ln_g[layer, 1], ln_b[layer, 1], moe_w)
    return (xp, xs,
            jnp.stack(kp), jnp.stack(vp), jnp.stack(ikp),
            jnp.stack(ks_), jnp.stack(vs_), jnp.stack(iks),
            jnp.stack(ssm_p), jnp.stack(ssm_s),
            jnp.stack(sconvp), jnp.stack(sconvs),
            jnp.stack(scp), jnp.stack(scs),
            jnp.stack(gvp), jnp.stack(gvs))
```

```python
import functools
import math

import jax
import jax.numpy as jnp
from jax import lax
from jax.experimental import pallas as pl
from jax.experimental.pallas import tpu as pltpu

F32 = jnp.float32
BF16 = jnp.bfloat16
I32 = jnp.int32

D_MODEL = 2048
DEPTH = 4
PAGE_SIZE = 128
HEAD_DIM = 128
ATT_HEADS = D_MODEL // HEAD_DIM
ATT_KV_HEADS = ATT_HEADS // 4
KV_REP = ATT_HEADS // ATT_KV_HEADS
IDX_HEADS = 16
IDX_DIM = 64
TOPK_MAX = 256
SSM_HEAD_DIM = 64
SSM_D = D_MODEL
SSM_HEADS = SSM_D // SSM_HEAD_DIM
SSM_GROUPS = 4
SSM_STATE = 128
SSM_CONV = 4
SSM_CHUNK = 128
SC_WIDTH = 3
GM_GROUPS = 16
GM_CHUNK = 128
N_EXPERTS = 32
TOP_K = 4
SWIGLU_LIMIT = 7.0
SWIGLU_ALPHA = 1.702
DN_ALPHA = (2 * DEPTH) ** 0.25
LN_EPS = 1e-5
RMS_EPS = 1e-5
A_KV = ATT_KV_HEADS * HEAD_DIM
BC_DIM = 2 * SSM_GROUPS * SSM_STATE
GROUP_COLS = SSM_D // SSM_GROUPS

LANES = 128
SUBLANES = 8
VMEM_LIMIT = 56 * 1024 * 1024

EV_Q, EV_Z, EV_XS, EV_QI, EV_BC, EV_K, EV_V, EV_KK, EV_WI, EV_DT, EV_N = (
    0, 2048, 4096, 6144, 7168, 8192, 8704, 9216, 9344, 9472, 9600)
EV_TN = 1920
OD_TN = 2048
DSA_SEG_BLOCKS = 2
OUTPROJ_TM = 256
MOE_TM = 1024
MOE_Q = 256
MOE_TN = 512
NEG_BIG = -1e30
INT_MIN = -2147483648


def _cparams(sem, **kw):
    return pltpu.CompilerParams(dimension_semantics=sem, vmem_limit_bytes=VMEM_LIMIT, **kw)


def _sds(shape, dtype=F32):
    return jax.ShapeDtypeStruct(shape, dtype)


def _silu(x):
    return x * jax.nn.sigmoid(x)


def _nt_dot(a, b):
    return lax.dot_general(a, b, (((1,), (1,)), ((), ())), preferred_element_type=F32)


def _dot(a, b):
    return jnp.dot(a, b, preferred_element_type=F32)


def _split3(x):
    hi = x.astype(BF16)
    r1 = x - hi.astype(F32)
    mid = r1.astype(BF16)
    lo = (r1 - mid.astype(F32)).astype(BF16)
    return hi, mid, lo


def _dot_exact_rhs(x, m_bf16):
    hi, mid, lo = _split3(x)
    return _dot(hi, m_bf16) + _dot(mid, m_bf16) + _dot(lo, m_bf16)


def _dot_exact_lhs(m_bf16, x):
    hi, mid, lo = _split3(x)
    return _dot(m_bf16, hi) + _dot(m_bf16, mid) + _dot(m_bf16, lo)


def _sort_key(x):
    b = pltpu.bitcast(x + 0.0, I32)
    return b ^ ((b >> 31) & jnp.int32(0x7FFFFFFF))


def _kth_largest_key(keys, k):
    rows = keys[0].shape[0]

    def body(it, p):
        cand = p | (jnp.int32(1) << (jnp.int32(31) - it))
        t = cand ^ jnp.int32(INT_MIN)
        cnt = jnp.zeros((rows, 1), I32)
        for kk in keys:
            cnt = cnt + jnp.sum((kk >= t).astype(I32), axis=-1, keepdims=True)
        return jnp.where(cnt >= k, cand, p)

    p = lax.fori_loop(0, 32, body, jnp.zeros((rows, 1), I32))
    return p ^ jnp.int32(INT_MIN)


def _adaln_kernel(c_ref, w_ref, b_ref, o_ref):
    s = _silu(c_ref[...]).astype(BF16)
    o_ref[...] = _dot(s, w_ref[...].astype(BF16)) + b_ref[...]


def adaln_all(c_all, ada_w, ada_b):
    depth, d, n = ada_w.shape
    rows = c_all.shape[0]
    tn = 1536
    return pl.pallas_call(
        _adaln_kernel,
        out_shape=_sds((depth, rows, n)),
        grid=(depth, n // tn),
        in_specs=[pl.BlockSpec((rows, d), lambda l, j: (0, 0)),
                  pl.BlockSpec((None, d, tn), lambda l, j: (l, 0, j)),
                  pl.BlockSpec((None, 1, tn), lambda l, j: (l, 0, j))],
        out_specs=pl.BlockSpec((None, rows, tn), lambda l, j: (l, 0, j)),
        compiler_params=_cparams(("arbitrary", "arbitrary")),
        name="adaln",
    )(c_all, ada_w, ada_b.reshape(depth, 1, n))


class Mod:
    def __init__(self, per_batch, rows_per_batch, expand):
        nb = per_batch.shape[0]
        self.rows_per_batch = rows_per_batch
        self.expand = expand
        if expand:
            m = per_batch.reshape(nb, 6, D_MODEL)
            self.rows = [jnp.repeat(m[:, w], rows_per_batch, axis=0) for w in range(6)]
        else:
            self.table = per_batch.reshape(nb * 6, 1, D_MODEL)

    def arg(self, which):
        return self.rows[which] if self.expand else self.table

    def spec(self, which, tm):
        if self.expand:
            return pl.BlockSpec((tm, D_MODEL), lambda i: (i, 0))
        rpb = self.rows_per_batch
        return pl.BlockSpec((None, 1, D_MODEL), lambda i: ((i * tm) // rpb * 6 + which, 0, 0))


def _row_tile(m, target):
    return min(m, target)


def _modulate_kernel(x_ref, sh_ref, sc_ref, o_ref):
    o_ref[...] = (x_ref[...] * (1.0 + sc_ref[...]) + sh_ref[...]).astype(o_ref.dtype)


def modulate(x, mod, out_dtype=BF16):
    m, d = x.shape
    tm = _row_tile(m, 512)
    return pl.pallas_call(
        _modulate_kernel,
        out_shape=_sds((m, d), out_dtype),
        grid=(m // tm,),
        in_specs=[pl.BlockSpec((tm, d), lambda i: (i, 0)), mod.spec(0, tm), mod.spec(1, tm)],
        out_specs=pl.BlockSpec((tm, d), lambda i: (i, 0)),
        compiler_params=_cparams(("arbitrary",)),
        name="modulate",
    )(x, mod.arg(0), mod.arg(1))


def _matmul_kernel(x_ref, w_ref, o_ref):
    o_ref[...] = _dot(x_ref[...], w_ref[...]).astype(o_ref.dtype)


def matmul(x, w, tn, out_dtype=F32):
    m, k = x.shape
    n = w.shape[1]
    tm = _row_tile(m, 512)
    return pl.pallas_call(
        _matmul_kernel,
        out_shape=_sds((m, n), out_dtype),
        grid=(n // tn, m // tm),
        in_specs=[pl.BlockSpec((tm, k), lambda j, i: (i, 0)),
                  pl.BlockSpec((k, tn), lambda j, i: (0, j))],
        out_specs=pl.BlockSpec((tm, tn), lambda j, i: (i, j)),
        compiler_params=_cparams(("arbitrary", "arbitrary")),
        name="in_proj",
    )(x, w)


def _layer_norm_rows(z, g, b):
    mu = jnp.mean(z, axis=-1, keepdims=True)
    zc = z - mu
    var = jnp.mean(zc * zc, axis=-1, keepdims=True)
    return zc * lax.rsqrt(var + LN_EPS) * g + b


def _top4_softmax(logits):
    tm = logits.shape[0]
    lane = lax.broadcasted_iota(I32, (tm, LANES), 1)
    v = logits
    vals, idxs = [], []
    for _ in range(TOP_K):
        m = jnp.max(v, axis=-1, keepdims=True)
        idx = jnp.min(jnp.where(v == m, lane, LANES), axis=-1, keepdims=True)
        vals.append(m)
        idxs.append(idx)
        v = jnp.where(lane == idx, -jnp.inf, v)
    es = [jnp.exp(m - vals[0]) for m in vals]
    den = es[0] + es[1] + es[2] + es[3]
    e_out = jnp.zeros((tm, LANES), I32)
    g_out = jnp.zeros((tm, LANES), F32)
    for k in range(TOP_K):
        e_out = jnp.where(lane == k, idxs[k], e_out)
        g_out = jnp.where(lane == k, es[k] / den, g_out)
    return e_out, g_out


def _outproj_kernel(ya_ref, yb_ref, wa_ref, wb_ref, x_ref, gate_ref, lng_ref, lnb_ref, sh_ref, sc_ref,
                    rw_ref, rb_ref, *rest, n_tiles, has_tail):
    if has_tail:
        tail_ref, xo_ref, h_ref, te_ref, tg_ref = rest
    else:
        xo_ref, h_ref, te_ref, tg_ref = rest

    def body():
        f = _dot(ya_ref[...], wa_ref[...]) + _dot(yb_ref[...], wb_ref[...])
        z = DN_ALPHA * x_ref[...] + (1.0 + gate_ref[...]) * f
        xn = _layer_norm_rows(z, lng_ref[...], lnb_ref[...])
        xo_ref[...] = xn
        h = xn * (1.0 + sc_ref[...]) + sh_ref[...]
        h_ref[...] = h
        logits = _dot(h.astype(BF16), rw_ref[...]) + rb_ref[...]
        e_out, g_out = _top4_softmax(logits)
        te_ref[...] = e_out
        tg_ref[...] = g_out

    if has_tail:
        i = pl.program_id(0)
        pl.when(i < n_tiles)(body)

        @pl.when(i == n_tiles)
        def _():
            h_ref[...] = tail_ref[...]
    else:
        body()


def outproj_ln_router(ya, yb, w_out, x, mod, ln_g, ln_b, router_w, router_b, tail=None):
    m, d = x.shape
    ka = ya.shape[1]
    tm = _row_tile(m, OUTPROJ_TM)
    n_tiles = m // tm
    has_tail = tail is not None
    wa = w_out[:ka].astype(BF16)
    wb = w_out[ka:].astype(BF16)
    rw = jnp.zeros((d, LANES), F32).at[:, :N_EXPERTS].set(router_w)
    rw = rw.astype(BF16)
    rb = jnp.full((1, LANES), NEG_BIG, F32).at[0, :N_EXPERTS].set(router_b)
    last = n_tiles - 1
    clamp = lambda spec: pl.BlockSpec(spec.block_shape, (lambda f: (lambda i: f(jnp.minimum(i, last))))(spec.index_map))
    const = lambda shape: pl.BlockSpec(shape, lambda i: (0,) * len(shape), pipeline_mode=pl.Buffered(1))
    row = lambda w: clamp(pl.BlockSpec((tm, w), lambda i: (i, 0)))
    in_specs = [row(ka), row(yb.shape[1]), const(wa.shape), const(wb.shape), row(d),
                clamp(mod.spec(2, tm)), const((1, d)), const((1, d)), clamp(mod.spec(3, tm)), clamp(mod.spec(4, tm)),
                const((d, LANES)), const((1, LANES))]
    args = [ya, yb, wa, wb, x, mod.arg(2), ln_g.reshape(1, d), ln_b.reshape(1, d), mod.arg(3), mod.arg(4),
            rw, rb]
    if has_tail:
        in_specs.append(const((tm, d)))
        args.append(tail)
    h_rows = m + (tm if has_tail else 0)
    return pl.pallas_call(
        functools.partial(_outproj_kernel, n_tiles=n_tiles, has_tail=has_tail),
        out_shape=(_sds((m, d)), _sds((h_rows, d)), _sds((m, LANES), I32), _sds((m, LANES))),
        grid=(n_tiles + (1 if has_tail else 0),),
        in_specs=in_specs,
        out_specs=(row(d), pl.BlockSpec((tm, d), lambda i: (i, 0)), row(LANES), row(LANES)),
        compiler_params=_cparams(("arbitrary",)),
        name="out_proj_ln_router",
    )(*args)


def _index_scores_block(qi_ref, wi_ref, kk):
    tq = qi_ref.shape[0]
    lane = lax.broadcasted_iota(I32, (tq, LANES), 1)
    wi = (wi_ref[...] * (IDX_HEADS ** -0.5)).astype(BF16).astype(F32)
    scores = None
    for p in range(IDX_HEADS // 2):
        qp = qi_ref[:, p * LANES:(p + 1) * LANES]
        for half in range(2):
            sel = (lane < IDX_DIM) if half == 0 else (lane >= IDX_DIM)
            qh = jnp.where(sel, qp, 0.0).astype(BF16)
            d = _nt_dot(qh, kk) * (IDX_DIM ** -0.5)
            h = 2 * p + half
            term = wi[:, h:h + 1] * jnp.maximum(d, 0.0).astype(BF16).astype(F32)
            scores = term if scores is None else scores + term
    return scores


def _dsa_prompt_kernel(q_ref, qi_ref, wi_ref, kk_ref, k_ref, v_ref, o_ref, *, n_sel, q0):
    tq = q_ref.shape[0]
    L = k_ref.shape[0]
    qb = pl.program_id(1)
    pos = q0 + qb * tq + lax.broadcasted_iota(I32, (tq, L), 0)
    kpos = lax.broadcasted_iota(I32, (tq, L), 1)
    valid = kpos <= pos
    if n_sel >= L:
        sel = valid
    else:
        scores = _index_scores_block(qi_ref, wi_ref, kk_ref[...].astype(BF16))
        key = jnp.where(valid, _sort_key(scores), _sort_key(jnp.full((1, 1), -jnp.inf, F32)))
        thr = _kth_largest_key([key], n_sel)
        sel = jnp.logical_and(key >= thr, valid)
    for g in range(ATT_KV_HEADS):
        kg = k_ref[:, g * HEAD_DIM:(g + 1) * HEAD_DIM].astype(BF16)
        vg = v_ref[:, g * HEAD_DIM:(g + 1) * HEAD_DIM].astype(BF16)
        for r in range(KV_REP):
            c0 = (g * KV_REP + r) * HEAD_DIM
            qh = q_ref[:, c0:c0 + HEAD_DIM].astype(BF16)
            logits = jnp.where(sel, _nt_dot(qh, kg) * (HEAD_DIM ** -0.5), NEG_BIG)
            m = jnp.max(logits, axis=-1, keepdims=True)
            p = jnp.exp(logits - m)
            l = jnp.sum(p, axis=-1, keepdims=True)
            o = _dot((p * (1.0 / l)).astype(BF16), vg)
            o_ref[:, c0:c0 + HEAD_DIM] = o.astype(o_ref.dtype)


def dsa_prompt(proj, n_batch, seq):
    tq = 128
    n_sel = min(TOPK_MAX, seq // 4)
    seg_blocks = min(DSA_SEG_BLOCKS, seq // tq)
    p3 = proj.reshape(n_batch, seq, proj.shape[1])
    outs = []
    for s in range(seq // (seg_blocks * tq)):
        q0 = s * seg_blocks * tq
        lc = q0 + seg_blocks * tq
        qrow = lambda w, col: pl.BlockSpec((None, tq, w), lambda b, i, s=s, col=col: (b, s * seg_blocks + i, col))
        keys = lambda w, col: pl.BlockSpec((None, lc, w), lambda b, i, col=col: (b, 0, col))
        outs.append(pl.pallas_call(
            functools.partial(_dsa_prompt_kernel, n_sel=n_sel, q0=q0),
            out_shape=_sds((n_batch, seg_blocks * tq, D_MODEL), BF16),
            grid=(n_batch, seg_blocks),
            in_specs=[qrow(D_MODEL, EV_Q // D_MODEL), qrow(1024, EV_QI // 1024), qrow(LANES, EV_WI // LANES),
                      keys(LANES, EV_KK // LANES), keys(A_KV, EV_K // A_KV), keys(A_KV, EV_V // A_KV)],
            out_specs=pl.BlockSpec((None, tq, D_MODEL), lambda b, i: (b, i, 0)),
            compiler_params=_cparams(("arbitrary", "arbitrary")),
            name="dsa_prompt",
        )(p3, p3, p3, p3, p3, p3))
    return jnp.concatenate(outs, axis=1).reshape(n_batch * seq, D_MODEL)


S1_PAGES = 16
S3_PAGES = 8


def _rows_index_scores(qs, wcol, keys_bf16):
    d = _nt_dot(qs, keys_bf16) * (IDX_DIM ** -0.5)
    w = wcol.astype(BF16).astype(F32) * jnp.maximum(d, 0.0).astype(BF16).astype(F32)
    nq = qs.shape[0] // IDX_HEADS
    return jnp.sum(w.reshape(nq, IDX_HEADS, w.shape[-1]), axis=1)


def _dsa_s1_kernel(pt_ref, qs_ref, wcol_ref, *refs):
    ik_refs, o_ref = refs[:S1_PAGES], refs[S1_PAGES]
    qs = qs_ref[...].astype(BF16)
    wcol = wcol_ref[...] * (IDX_HEADS ** -0.5)
    for i in range(S1_PAGES):
        o_ref[:, i * PAGE_SIZE:(i + 1) * PAGE_SIZE] = _rows_index_scores(qs, wcol, ik_refs[i][...].astype(BF16))


def dsa_sample_scores(page_table, qs_rows, wcol, pool_ik, layer_j):
    nb, n_pages = page_table.shape
    nq = qs_rows.shape[1] // IDX_HEADS
    steps = n_pages // S1_PAGES

    def ik_spec(i):
        return pl.BlockSpec((None, None, PAGE_SIZE, IDX_DIM),
                            lambda b, c, pt: (layer_j, pt[b, c * S1_PAGES + i], 0, 0))

    return pl.pallas_call(
        _dsa_s1_kernel,
        out_shape=_sds((nb, nq, n_pages * PAGE_SIZE)),
        grid_spec=pltpu.PrefetchScalarGridSpec(
            num_scalar_prefetch=1, grid=(nb, steps),
            in_specs=[pl.BlockSpec((None, nq * IDX_HEADS, IDX_DIM), lambda b, c, pt: (b, 0, 0)),
                      pl.BlockSpec((None, nq * IDX_HEADS, 1), lambda b, c, pt: (b, 0, 0))]
            + [ik_spec(i) for i in range(S1_PAGES)],
            out_specs=pl.BlockSpec((None, nq, S1_PAGES * PAGE_SIZE), lambda b, c, pt: (b, 0, c))),
        compiler_params=_cparams(("arbitrary", "arbitrary")),
        name="dsa_sample_scores",
    )(page_table, qs_rows, wcol, *([pool_ik] * S1_PAGES))


def _dsa_s3_kernel(pt_ref, q_ref, qs_ref, wcol_ref, kinew_ref, sall_ref, sblk_ref, knew_ref, vnew_ref, *refs,
                   n_sel, n_steps):
    kp = refs[:S3_PAGES]
    vp = refs[S3_PAGES:2 * S3_PAGES]
    o_ref = refs[2 * S3_PAGES]
    thr_s, newkey_s, m_s, l_s, acc_s = refs[2 * S3_PAGES + 1:]
    c = pl.program_id(1)
    nq = q_ref.shape[0]
    rows = KV_REP * nq
    neg_inf_key = _sort_key(jnp.full((1, 1), -jnp.inf, F32))

    def new_valid():
        t = lax.broadcasted_iota(I32, (nq, LANES), 0)
        s = lax.broadcasted_iota(I32, (nq, LANES), 1)
        return jnp.logical_and(s <= t, s < nq)

    @pl.when(c == 0)
    def _():
        wcol = wcol_ref[...] * (IDX_HEADS ** -0.5)
        s_new = _rows_index_scores(qs_ref[...].astype(BF16), wcol, kinew_ref[...].astype(BF16))
        nk = jnp.where(new_valid(), _sort_key(s_new), neg_inf_key)
        newkey_s[...] = nk
        thr_s[...] = _kth_largest_key([_sort_key(sall_ref[...]), nk], n_sel)
        m_s[...] = jnp.full(m_s.shape, NEG_BIG, F32)
        l_s[...] = jnp.zeros(l_s.shape, F32)
        acc_s[...] = jnp.zeros(acc_s.shape, F32)

    thr = thr_s[...]
    qg = [jnp.concatenate([q_ref[:, (g * KV_REP + r) * HEAD_DIM:(g * KV_REP + r + 1) * HEAD_DIM]
                           for r in range(KV_REP)], axis=0).astype(BF16) for g in range(ATT_KV_HEADS)]

    def update(g, kg, vg, sel_rows):
        logits = _nt_dot(qg[g], kg) * (HEAD_DIM ** -0.5)
        lm = jnp.where(sel_rows, logits, NEG_BIG)
        m_old = m_s[g]
        m_new = jnp.maximum(m_old, jnp.max(lm, axis=-1, keepdims=True))
        p = jnp.where(sel_rows, jnp.exp(lm - m_new), 0.0)
        a = jnp.exp(m_old - m_new)
        l_s[g] = a * l_s[g] + jnp.sum(p, axis=-1, keepdims=True)
        acc_s[g] = a * acc_s[g] + _dot(p.astype(BF16), vg)
        m_s[g] = m_new

    def tile_rows(sel):
        f = jnp.where(sel, 1.0, 0.0)
        return jnp.concatenate([f] * KV_REP, axis=0) > 0.5

    def head_rows(page_refs, g):
        return jnp.concatenate([r[pl.ds(g, PAGE_SIZE, stride=ATT_KV_HEADS), :] for r in page_refs],
                               axis=0).astype(BF16)

    sel = tile_rows(_sort_key(sblk_ref[...]) >= thr)
    for g in range(ATT_KV_HEADS):
        update(g, head_rows(kp, g), head_rows(vp, g), sel)

    @pl.when(c == n_steps - 1)
    def _():
        sel = tile_rows(jnp.logical_and(newkey_s[...] >= thr, new_valid()))
        for g in range(ATT_KV_HEADS):
            update(g, knew_ref[:, g * HEAD_DIM:(g + 1) * HEAD_DIM].astype(BF16),
                   vnew_ref[:, g * HEAD_DIM:(g + 1) * HEAD_DIM].astype(BF16), sel)
        for g in range(ATT_KV_HEADS):
            o = acc_s[g] / l_s[g]
            for r in range(KV_REP):
                c0 = (g * KV_REP + r) * HEAD_DIM
                o_ref[:, c0:c0 + HEAD_DIM] = o[r * nq:(r + 1) * nq].astype(o_ref.dtype)


def dsa_sample_attend(page_table, q, qs_rows, wcol, ki_new, scores, k_new, v_new, pool_k, pool_v, layer_j):
    nb, n_pages = page_table.shape
    nq = q.shape[1]
    past = n_pages * PAGE_SIZE
    n_sel = min(TOPK_MAX, (past + nq) // 4)
    n_steps = n_pages // S3_PAGES
    rows = KV_REP * nq

    def page_spec(i):
        return pl.BlockSpec((None, None, PAGE_SIZE * ATT_KV_HEADS, HEAD_DIM),
                            lambda b, c, pt: (layer_j, pt[b, c * S3_PAGES + i], 0, 0))

    pool_k = pool_k.reshape(pool_k.shape[0], pool_k.shape[1], PAGE_SIZE * ATT_KV_HEADS, HEAD_DIM)
    pool_v = pool_v.reshape(pool_v.shape[0], pool_v.shape[1], PAGE_SIZE * ATT_KV_HEADS, HEAD_DIM)
    per_b = lambda *shape: pl.BlockSpec((None,) + shape, lambda b, c, pt: (b,) + (0,) * len(shape))
    return pl.pallas_call(
        functools.partial(_dsa_s3_kernel, n_sel=n_sel, n_steps=n_steps),
        out_shape=_sds((nb, nq, D_MODEL), BF16),
        grid_spec=pltpu.PrefetchScalarGridSpec(
            num_scalar_prefetch=1, grid=(nb, n_steps),
            in_specs=[per_b(nq, D_MODEL), per_b(nq * IDX_HEADS, IDX_DIM), per_b(nq * IDX_HEADS, 1),
                      per_b(LANES, IDX_DIM), per_b(nq, past),
                      pl.BlockSpec((None, nq, S3_PAGES * PAGE_SIZE), lambda b, c, pt: (b, 0, c)),
                      per_b(LANES, A_KV), per_b(LANES, A_KV)]
            + [page_spec(i) for i in range(S3_PAGES)] * 2,
            out_specs=per_b(nq, D_MODEL),
            scratch_shapes=[pltpu.VMEM((nq, 1), I32), pltpu.VMEM((nq, LANES), I32),
                            pltpu.VMEM((ATT_KV_HEADS, rows, 1), F32), pltpu.VMEM((ATT_KV_HEADS, rows, 1), F32),
                            pltpu.VMEM((ATT_KV_HEADS, rows, HEAD_DIM), F32)]),
        compiler_params=_cparams(("arbitrary", "arbitrary")),
        name="dsa_sample_attend",
    )(page_table, q, qs_rows, wcol, ki_new, scores, scores, k_new, v_new,
      *([pool_k] * S3_PAGES), *([pool_v] * S3_PAGES))


def _conv_rows(buf, w_ref, rows, width):
    base = SUBLANES - (width - 1)
    out = None
    for j in range(width):
        term = w_ref[j:j + 1, :] * buf[base + j:base + j + rows, :]
        out = term if out is None else out + term
    return out


def _ssd_kernel(z_ref, xs_ref, bc_ref, dt_ref, prevx_ref, prevb_ref, h0_ref,
                cwx_ref, cwb_ref, cbx_ref, cbb_ref, dtb_ref, alog_ref, dskip_ref, nw_ref, e_ref,
                y_ref, hl_ref, bufx, bufb, xs_s, bc_s, st_s, *, rows, n_chunks):
    c = pl.program_id(1)
    ck = SSM_CHUNK

    @pl.when(c == 0)
    def _():
        bufx[0:SUBLANES] = prevx_ref[...]
        bufb[0:SUBLANES] = prevb_ref[...]
        st_s[...] = h0_ref[...]

    @pl.when(c > 0)
    def _():
        bufx[0:SUBLANES] = bufx[rows:rows + SUBLANES]
        bufb[0:SUBLANES] = bufb[rows:rows + SUBLANES]

    bufx[SUBLANES:SUBLANES + rows] = xs_ref[...]
    bufb[SUBLANES:SUBLANES + rows] = bc_ref[...]
    xc = _silu(_conv_rows(bufx, cwx_ref, rows, SSM_CONV) + cbx_ref[...])
    bcc = _silu(_conv_rows(bufb, cwb_ref, rows, SSM_CONV) + cbb_ref[...])
    if rows < ck:
        xs_s[...] = jnp.zeros(xs_s.shape, F32)
        bc_s[...] = jnp.zeros(bc_s.shape, F32)
    xs_s[0:rows] = xc
    bc_s[0:rows] = bcc

    dt_raw = dt_ref[...] + dtb_ref[...]
    dt = jnp.maximum(dt_raw, 0.0) + jnp.log1p(jnp.exp(-jnp.abs(dt_raw)))
    if rows < ck:
        dt = jnp.concatenate([dt, jnp.zeros((ck - rows, LANES), F32)], axis=0)
    a = -jnp.exp(alog_ref[...])
    ri = lax.broadcasted_iota(I32, (ck, ck), 0)
    ci = lax.broadcasted_iota(I32, (ck, ck), 1)
    tril = ri >= ci
    tril_b = jnp.where(tril, 1.0, 0.0).astype(BF16)
    acum = _dot_exact_lhs(tril_b, dt * a)
    acum_t = acum.T
    e = e_ref[...]
    dtx = _dot_exact_rhs(dt, e)
    acx = _dot_exact_rhs(acum, e)
    eax = jnp.exp(acx)
    last = acx[ck - 1:ck, :]
    dendx = jnp.exp(last - acx)
    cdecx = jnp.exp(last)

    xs = xs_s[...]
    xdt = xs * dtx
    xdt_b = xdt.astype(BF16)
    xdd_b = (xdt * dendx).astype(BF16)
    lane = lax.broadcasted_iota(I32, (ck, LANES), 1)
    lo_half = lane < SSM_HEAD_DIM
    hpg = SSM_HEADS // SSM_GROUPS
    y_parts = []
    for g in range(SSM_GROUPS):
        bm = bc_s[:, g * SSM_STATE:(g + 1) * SSM_STATE]
        cm = bc_s[:, (SSM_GROUPS + g) * SSM_STATE:(SSM_GROUPS + g + 1) * SSM_STATE]
        bm_b = bm.astype(BF16)
        cm_b = cm.astype(BF16)
        cb = _nt_dot(cm_b, bm_b)
        g0 = g * GROUP_COLS
        st = st_s[g]
        y_off = _dot(cm_b, st.astype(BF16)) * eax[:, g0:g0 + GROUP_COLS]
        pair_out = []
        for pr in range(hpg // 2):
            res = []
            for half in range(2):
                h = g * hpg + 2 * pr + half
                seg = acum[:, h:h + 1] - acum_t[h:h + 1, :]
                decay = jnp.exp(jnp.where(tril, seg, -jnp.inf))
                sc = (cb * decay).astype(BF16)
                res.append(_dot(sc, xdt_b[:, g0 + pr * LANES:g0 + (pr + 1) * LANES]))
            pair_out.append(jnp.where(lo_half, res[0], res[1]))
        y_parts.append(jnp.concatenate(pair_out, axis=1) + y_off)
        new_st = _dot(bm.T.astype(BF16), xdd_b[:, g0:g0 + GROUP_COLS])
        st_s[g] = st * cdecx[:, g0:g0 + GROUP_COLS] + new_st

    zz = z_ref[...]
    nw = nw_ref[...]
    for g in range(SSM_GROUPS):
        g0 = g * GROUP_COLS
        y = y_parts[g][0:rows] + dskip_ref[:, g0:g0 + GROUP_COLS] * xs[0:rows, g0:g0 + GROUP_COLS]
        y = y * _silu(zz[:, g0:g0 + GROUP_COLS])
        y = y * lax.rsqrt(jnp.mean(y * y, axis=-1, keepdims=True) + RMS_EPS)
        y_ref[:, g0:g0 + GROUP_COLS] = (y * nw[:, g0:g0 + GROUP_COLS]).astype(y_ref.dtype)

    @pl.when(c == n_chunks - 1)
    def _():
        hl_ref[...] = st_s[...]


def _pad_prev(prev):
    b, w1, ch = prev.shape
    return jnp.concatenate([jnp.zeros((b, SUBLANES - w1, ch), prev.dtype), prev], axis=1)


def _pad_rows(w, rows=SUBLANES):
    return jnp.concatenate([w, jnp.zeros((rows - w.shape[0], w.shape[1]), w.dtype)], axis=0)


def _pad_lanes(v, n=LANES):
    return jnp.zeros((1, n), v.dtype).at[0, :v.shape[0]].set(v)


def ssd_mixer(proj, n_batch, seq, conv_prev, h0, conv_w, conv_b, dt_bias, a_log, d_skip, norm_w):
    rows = SSM_CHUNK if seq % SSM_CHUNK == 0 else seq
    n_chunks = seq // rows
    hpg = SSM_HEADS // SSM_GROUPS
    prev = _pad_prev(conv_prev)
    prevx, prevb = prev[:, :, :SSM_D], prev[:, :, SSM_D:]
    h0k = h0.reshape(n_batch, SSM_GROUPS, hpg, SSM_HEAD_DIM, SSM_STATE).transpose(0, 1, 4, 2, 3)
    h0k = h0k.reshape(n_batch, SSM_GROUPS, SSM_STATE, GROUP_COLS)
    cw = _pad_rows(conv_w)
    e = (jnp.arange(LANES)[:, None] == (jnp.arange(SSM_D)[None, :] // SSM_HEAD_DIM)).astype(BF16)
    nb = n_chunks
    row = lambda w, col: pl.BlockSpec((rows, w), lambda b, c: (b * nb + c, col))
    per_b = lambda *shape: pl.BlockSpec((None,) + shape, lambda b, c: (b,) + (0,) * len(shape))
    const = lambda *shape: pl.BlockSpec(shape, lambda b, c: (0,) * len(shape))
    y, hl = pl.pallas_call(
        functools.partial(_ssd_kernel, rows=rows, n_chunks=n_chunks),
        out_shape=(_sds((n_batch * seq, SSM_D), BF16),
                   _sds((n_batch, SSM_GROUPS, SSM_STATE, GROUP_COLS))),
        grid=(n_batch, n_chunks),
        in_specs=[row(SSM_D, EV_Z // SSM_D), row(SSM_D, EV_XS // SSM_D), row(BC_DIM, EV_BC // BC_DIM),
                  row(LANES, EV_DT // LANES),
                  per_b(SUBLANES, SSM_D), per_b(SUBLANES, BC_DIM), per_b(SSM_GROUPS, SSM_STATE, GROUP_COLS),
                  const(SUBLANES, SSM_D), const(SUBLANES, BC_DIM), const(1, SSM_D), const(1, BC_DIM),
                  const(1, LANES), const(1, LANES), const(1, SSM_D), const(1, SSM_D), const(LANES, SSM_D)],
        out_specs=(pl.BlockSpec((rows, SSM_D), lambda b, c: (b * nb + c, 0)),
                   per_b(SSM_GROUPS, SSM_STATE, GROUP_COLS)),
        scratch_shapes=[pltpu.VMEM((rows + SUBLANES, SSM_D), F32), pltpu.VMEM((rows + SUBLANES, BC_DIM), F32),
                        pltpu.VMEM((SSM_CHUNK, SSM_D), F32), pltpu.VMEM((SSM_CHUNK, BC_DIM), F32),
                        pltpu.VMEM((SSM_GROUPS, SSM_STATE, GROUP_COLS), F32)],
        compiler_params=_cparams(("arbitrary", "arbitrary")),
        name="ssd_mixer",
    )(proj, proj, proj, proj, prevx, prevb, h0k,
      cw[:, :SSM_D], cw[:, SSM_D:], conv_b[None, :SSM_D], conv_b[None, SSM_D:],
      _pad_lanes(dt_bias), _pad_lanes(a_log), jnp.repeat(d_skip, SSM_HEAD_DIM)[None, :], norm_w[None, :], e)
    h_last = hl.reshape(n_batch, SSM_GROUPS, SSM_STATE, hpg, SSM_HEAD_DIM).transpose(0, 1, 3, 4, 2)
    return y, h_last.reshape(n_batch, SSM_HEADS, SSM_HEAD_DIM, SSM_STATE)


def _odd_kernel(bg_ref, cg_ref, sv_ref, u_ref, v_ref, prev_ref, scw_ref, lng_ref, lnb_ref, ws_ref, bst_ref,
                sc_ref, gm_ref, scst_ref, gmv_ref, buf, vpad, *, rows, n_chunks):
    c = pl.program_id(1)
    ck = GM_CHUNK

    @pl.when(c == 0)
    def _():
        buf[0:SUBLANES] = prev_ref[...]

    @pl.when(c > 0)
    def _():
        buf[0:SUBLANES] = buf[rows:rows + SUBLANES]

    buf[SUBLANES:SUBLANES + rows] = cg_ref[...] * sv_ref[...]
    sc_ref[...] = (bg_ref[...] * _conv_rows(buf, scw_ref, rows, SC_WIDTH)).astype(sc_ref.dtype)

    def gelu(x):
        return 0.5 * x * (1.0 + lax.erf(x * (0.5 ** 0.5)))

    u = gelu(u_ref[...])
    v = _layer_norm_rows(gelu(v_ref[...]), lng_ref[...], lnb_ref[...])
    if rows < ck:
        vpad[...] = jnp.zeros(vpad.shape, F32)
    vpad[0:rows] = v
    ri = lax.broadcasted_iota(I32, (ck, ck), 0)
    ci = lax.broadcasted_iota(I32, (ck, ck), 1)
    tril = ri >= ci
    gd = D_MODEL // GM_GROUPS
    for g in range(GM_GROUPS):
        wm = jnp.where(tril, ws_ref[g], 0.0).astype(BF16)
        mixed = _dot(wm, vpad[:, g * gd:(g + 1) * gd].astype(BF16)) + bst_ref[:, g:g + 1]
        gm_ref[:, g * gd:(g + 1) * gd] = (u[:, g * gd:(g + 1) * gd] * mixed[0:rows]).astype(gm_ref.dtype)

    @pl.when(c == n_chunks - 1)
    def _():
        scst_ref[...] = buf[rows:rows + SUBLANES]
        gmv_ref[...] = v


def odd_mixer(proj, n_batch, seq, sconv_prev, sconv_w, gm_ln_g, gm_ln_b, gm_ws, gm_bs):
    d = D_MODEL
    rows = GM_CHUNK if seq % GM_CHUNK == 0 else seq
    n_chunks = seq // rows
    nb = n_chunks
    prev = _pad_prev(sconv_prev)
    bst = jnp.zeros((GM_CHUNK, LANES), F32).at[:, :GM_GROUPS].set(gm_bs.T)
    row = lambda col: pl.BlockSpec((rows, d), lambda b, c: (b * nb + c, col))
    per_b = lambda *shape: pl.BlockSpec((None,) + shape, lambda b, c: (b,) + (0,) * len(shape))
    const = lambda *shape: pl.BlockSpec(shape, lambda b, c: (0,) * len(shape))
    sc, gm, scst, gmv = pl.pallas_call(
        functools.partial(_odd_kernel, rows=rows, n_chunks=n_chunks),
        out_shape=(_sds((n_batch * seq, d), BF16), _sds((n_batch * seq, d), BF16),
                   _sds((n_batch, SUBLANES, d)), _sds((n_batch, rows, d))),
        grid=(n_batch, n_chunks),
        in_specs=[row(0), row(1), row(2), row(3), row(4), per_b(SUBLANES, d), const(SUBLANES, d),
                  const(1, d), const(1, d), const(GM_GROUPS, GM_CHUNK, GM_CHUNK), const(GM_CHUNK, LANES)],
        out_specs=(pl.BlockSpec((rows, d), lambda b, c: (b * nb + c, 0)),
                   pl.BlockSpec((rows, d), lambda b, c: (b * nb + c, 0)),
                   per_b(SUBLANES, d), per_b(rows, d)),
        scratch_shapes=[pltpu.VMEM((rows + SUBLANES, d), F32), pltpu.VMEM((GM_CHUNK, d), F32)],
        compiler_params=_cparams(("arbitrary", "arbitrary")),
        name="odd_mixer",
    )(proj, proj, proj, proj, proj, prev, _pad_rows(sconv_w), gm_ln_g[None, :], gm_ln_b[None, :], gm_ws, bst)
    return sc, gm, scst[:, SUBLANES - (SC_WIDTH - 1):], gmv


def _row_copy(src_hbm, row, dst, slot, sem):
    return pltpu.make_async_copy(src_hbm.at[pl.ds(row, 1), :], dst.at[pl.ds(slot, 1), :], sem)


DISPATCH_UNROLL = 8


def _dispatch_kernel(live_ref, src_ref, h_hbm, o_ref, buf, sem):
    i = pl.program_id(0)
    tq = buf.shape[0]

    @pl.when(live_ref[i] > 0)
    def _():
        def issue(t8, carry):
            for u in range(DISPATCH_UNROLL):
                t = t8 * DISPATCH_UNROLL + u
                _row_copy(h_hbm, src_ref[0, t], buf, t, sem).start()
            return carry

        lax.fori_loop(0, tq // DISPATCH_UNROLL, issue, 0)
        pltpu.make_async_copy(h_hbm.at[pl.ds(0, tq), :], buf, sem).wait()
        o_ref[...] = buf[...].astype(o_ref.dtype)

    @pl.when(live_ref[i] == 0)
    def _():
        o_ref[...] = jnp.zeros(o_ref.shape, o_ref.dtype)


def moe_dispatch(row_src, sub_live, h_all):
    d = h_all.shape[1]
    tq = MOE_Q
    n_sub = sub_live.shape[0]
    return pl.pallas_call(
        _dispatch_kernel,
        out_shape=_sds((n_sub * tq, d), BF16),
        grid_spec=pltpu.PrefetchScalarGridSpec(
            num_scalar_prefetch=1, grid=(n_sub,),
            in_specs=[pl.BlockSpec((None, 1, tq), lambda i, lv: (i, 0, 0), memory_space=pltpu.SMEM),
                      pl.BlockSpec(memory_space=pl.ANY)],
            out_specs=pl.BlockSpec((tq, d), lambda i, lv: (i, 0)),
            scratch_shapes=[pltpu.VMEM((tq, d), F32), pltpu.SemaphoreType.DMA(())]),
        compiler_params=_cparams(("arbitrary",), disable_bounds_checks=True),
        name="moe_dispatch",
    )(sub_live, row_src.reshape(n_sub, 1, tq), h_all)


def _first_of_expert(be_ref, i):
    prev = be_ref[jnp.maximum(i - 1, 0)]
    return jnp.logical_or(i == 0, be_ref[i] != prev)


def _for_live_rows(nsub_ref, i, o_ref, compute):
    n_sub_max = o_ref.shape[0] // MOE_Q
    ns = nsub_ref[i]
    for q in range(1, n_sub_max + 1):
        @pl.when(ns == q)
        def _(q=q):
            rows = q * MOE_Q
            o_ref[0:rows] = compute(rows).astype(o_ref.dtype)
            if q < n_sub_max:
                o_ref[rows:] = jnp.zeros((o_ref.shape[0] - rows, o_ref.shape[1]), o_ref.dtype)

    @pl.when(ns == 0)
    def _():
        o_ref[...] = jnp.zeros(o_ref.shape, o_ref.dtype)


def _expert_gu_kernel(be_ref, nsub_ref, nl_ref, x_ref, wg_ref, wu_ref, bg_ref, bu_ref, o_ref, wg_s, wu_s):
    i = pl.program_id(1)

    @pl.when(_first_of_expert(be_ref, i))
    def _():
        wg_s[...] = wg_ref[...].astype(BF16)
        wu_s[...] = wu_ref[...].astype(BF16)

    def compute(rows):
        x = x_ref[0:rows]
        g = _dot(x, wg_s[...]) + bg_ref[...]
        u = _dot(x, wu_s[...]) + bu_ref[...]
        g = jnp.minimum(g, SWIGLU_LIMIT)
        u = jnp.clip(u, -SWIGLU_LIMIT, SWIGLU_LIMIT)
        return (u + 1.0) * (g * jax.nn.sigmoid(SWIGLU_ALPHA * g))

    _for_live_rows(nsub_ref, i, o_ref, compute)


def _expert_down_kernel(be_ref, nsub_ref, nl_ref, a_ref, w_ref, b_ref, o_ref, w_s):
    i = pl.program_id(1)

    @pl.when(_first_of_expert(be_ref, i))
    def _():
        w_s[...] = w_ref[...].astype(BF16)

    _for_live_rows(nsub_ref, i, o_ref, lambda rows: _dot(a_ref[0:rows], w_s[...]) + b_ref[...])


def moe_experts(xs, block_e, block_nsub, n_live, w_gu, b_gu, w_down, b_down, layer):
    n_rows, d = xs.shape
    tm, tn = MOE_TM, MOE_TN
    n_blocks = n_rows // tm
    de = w_down.shape[2]
    nj = de // tn
    live = lambda i, nl: jnp.minimum(i, nl[0] - 1)
    act = pl.pallas_call(
        _expert_gu_kernel,
        out_shape=_sds((n_rows, de), BF16),
        grid_spec=pltpu.PrefetchScalarGridSpec(
            num_scalar_prefetch=3, grid=(nj, n_blocks),
            in_specs=[pl.BlockSpec((tm, d), lambda j, i, be, ns, nl: (live(i, nl), 0)),
                      pl.BlockSpec((None, None, d, tn), lambda j, i, be, ns, nl: (layer, be[i], 0, j)),
                      pl.BlockSpec((None, None, d, tn), lambda j, i, be, ns, nl: (layer, be[i], 0, nj + j)),
                      pl.BlockSpec((None, None, 1, tn), lambda j, i, be, ns, nl: (layer, be[i], 0, j)),
                      pl.BlockSpec((None, None, 1, tn), lambda j, i, be, ns, nl: (layer, be[i], 0, nj + j))],
            out_specs=pl.BlockSpec((tm, tn), lambda j, i, be, ns, nl: (i, j)),
            scratch_shapes=[pltpu.VMEM((d, tn), BF16), pltpu.VMEM((d, tn), BF16)]),
        compiler_params=_cparams(("arbitrary", "arbitrary")),
        name="moe_expert_gu",
    )(block_e, block_nsub, n_live, xs, w_gu, w_gu, b_gu.reshape(b_gu.shape[0], b_gu.shape[1], 1, -1),
      b_gu.reshape(b_gu.shape[0], b_gu.shape[1], 1, -1))
    dn = w_down.shape[3]
    njd = dn // tn
    return pl.pallas_call(
        _expert_down_kernel,
        out_shape=_sds((n_rows, dn)),
        grid_spec=pltpu.PrefetchScalarGridSpec(
            num_scalar_prefetch=3, grid=(njd, n_blocks),
            in_specs=[pl.BlockSpec((tm, de), lambda j, i, be, ns, nl: (live(i, nl), 0)),
                      pl.BlockSpec((None, None, de, tn), lambda j, i, be, ns, nl: (layer, be[i], 0, j)),
                      pl.BlockSpec((None, None, 1, tn), lambda j, i, be, ns, nl: (layer, be[i], 0, j))],
            out_specs=pl.BlockSpec((tm, tn), lambda j, i, be, ns, nl: (i, j)),
            scratch_shapes=[pltpu.VMEM((de, tn), BF16)]),
        compiler_params=_cparams(("arbitrary", "arbitrary")),
        name="moe_expert_down",
    )(block_e, block_nsub, n_live, act, w_down, b_down.reshape(b_down.shape[0], b_down.shape[1], 1, -1))


def _combine_kernel(dest_ref, y_hbm, x_ref, tg_ref, gate_ref, lng_ref, lnb_ref, *rest, tok0, has_next):
    if has_next:
        sh_ref, sc_ref, xo_ref, ho_ref, buf, sem = rest
    else:
        xo_ref, buf, sem = rest
    i = pl.program_id(0)
    tm = x_ref.shape[0]

    def issue(t2, carry):
        for u in range(2):
            t = t2 * 2 + u
            base = (tok0 + i * tm + t) * TOP_K
            for k in range(TOP_K):
                _row_copy(y_hbm, dest_ref[base + k], buf.at[k], t, sem).start()
        return carry

    lax.fori_loop(0, tm // 2, issue, 0)
    for k in range(TOP_K):
        pltpu.make_async_copy(y_hbm.at[pl.ds(0, tm), :], buf.at[k], sem).wait()
    tg = tg_ref[...]
    f = tg[:, 0:1] * buf[0]
    for k in range(1, TOP_K):
        f = f + tg[:, k:k + 1] * buf[k]
    z = DN_ALPHA * x_ref[...] + (1.0 + gate_ref[...]) * f
    xn = _layer_norm_rows(z, lng_ref[...], lnb_ref[...])
    xo_ref[...] = xn
    if has_next:
        ho_ref[...] = (xn * (1.0 + sc_ref[...]) + sh_ref[...]).astype(ho_ref.dtype)


def moe_combine_ln(dest, y, x, top_g, mod, ln_g, ln_b, next_mod, tok0):
    m, d = x.shape
    tm = _row_tile(m, 256)
    has_next = next_mod is not None
    row = lambda w: pl.BlockSpec((tm, w), lambda i, dst: (i, 0))
    const = lambda *shape: pl.BlockSpec(shape, lambda i, dst: (0,) * len(shape))
    lift = lambda spec: pl.BlockSpec(spec.block_shape, (lambda f: (lambda i, dst: f(i)))(spec.index_map))
    in_specs = [pl.BlockSpec(memory_space=pl.ANY), row(d), row(LANES), lift(mod.spec(5, tm)), const(1, d), const(1, d)]
    args = [dest, y, x, top_g, mod.arg(5), ln_g.reshape(1, d), ln_b.reshape(1, d)]
    out_shape = [_sds((m, d))]
    out_specs = [row(d)]
    if has_next:
        in_specs += [lift(next_mod.spec(0, tm)), lift(next_mod.spec(1, tm))]
        args += [next_mod.arg(0), next_mod.arg(1)]
        out_shape.append(_sds((m, d), BF16))
        out_specs.append(row(d))
    out = pl.pallas_call(
        functools.partial(_combine_kernel, tok0=tok0, has_next=has_next),
        out_shape=tuple(out_shape),
        grid_spec=pltpu.PrefetchScalarGridSpec(
            num_scalar_prefetch=1, grid=(m // tm,),
            in_specs=in_specs, out_specs=tuple(out_specs),
            scratch_shapes=[pltpu.VMEM((TOP_K, tm, d), F32), pltpu.SemaphoreType.DMA(())]),
        compiler_params=_cparams(("arbitrary",), disable_bounds_checks=True),
        name="moe_combine_ln",
    )(*args)
    return (out[0], out[1]) if has_next else (out[0], None)


def moe_routing_tables(top_e):
    tm, tq = MOE_TM, MOE_Q
    n_tok = top_e.shape[0]
    nk = n_tok * TOP_K
    n_blocks = -(-nk // tm) + N_EXPERTS
    flat_e = top_e.reshape(-1)
    onehot = (flat_e[:, None] == jnp.arange(N_EXPERTS, dtype=I32)[None, :]).astype(I32)
    csum = jnp.cumsum(onehot, axis=0)
    counts = csum[-1]
    rank = jnp.sum(csum * onehot, axis=1) - 1
    blocks_per_e = (counts + tm - 1) // tm
    block_end = jnp.cumsum(blocks_per_e)
    block_start = block_end - blocks_per_e
    dest = (block_start[flat_e] * tm + rank).astype(I32)
    row_src = jnp.zeros((n_blocks * tm,), I32).at[dest].set(jnp.arange(nk, dtype=I32) // TOP_K)
    blk = jnp.arange(n_blocks, dtype=I32)
    n_live = block_end[-1].astype(I32)
    block_e = jnp.minimum(jnp.sum((blk[:, None] >= block_end[None, :]).astype(I32), axis=1), N_EXPERTS - 1)
    rows_left = counts[block_e] - (blk - block_start[block_e]) * tm
    block_nsub = jnp.where(blk < n_live, jnp.clip((rows_left + tq - 1) // tq, 0, tm // tq), 0).astype(I32)
    block_e = jnp.where(blk < n_live, block_e, block_e[n_live - 1]).astype(I32)
    sub = jnp.arange(n_blocks * (tm // tq), dtype=I32)
    sub_live = (sub % (tm // tq) < block_nsub[sub // (tm // tq)]).astype(I32)
    return dest, row_src, block_e, block_nsub, sub_live, n_live.reshape(1)


def moe_sublayer(xs, h_all, tes, tgs, mods, next_mods, ln_g, ln_b, w_gu, b_gu, w_down, b_down, layer):
    top_e = jnp.concatenate([t[:, :TOP_K] for t in tes], axis=0)
    dest, row_src, block_e, block_nsub, sub_live, n_live = moe_routing_tables(top_e)
    xsorted = moe_dispatch(row_src, sub_live, h_all)
    y = moe_experts(xsorted, block_e, block_nsub, n_live, w_gu, b_gu, w_down, b_down, layer)
    outs, nexts = [], []
    tok0 = 0
    for grp in range(2):
        xo, hn = moe_combine_ln(dest, y, xs[grp], tgs[grp], mods[grp], ln_g, ln_b,
                                None if next_mods is None else next_mods[grp], tok0)
        tok0 += xs[grp].shape[0]
        outs.append(xo)
        nexts.append(hn)
    return outs, nexts


def _even_w_in(w):
    o = [0]
    for s in (2048, 512, 512, 1024, 64, 16, 2048, 3072, 32):
        o.append(o[-1] + s)
    q, k, v, qi, ki, wi, z, xbc, dt = [w[:, o[i]:o[i + 1]] for i in range(9)]
    pad = lambda t, n: jnp.concatenate([t, jnp.zeros((t.shape[0], n - t.shape[1]), t.dtype)], axis=1)
    cols = [q, z, xbc[:, :SSM_D], qi, xbc[:, SSM_D:], k, v, ki, ki, pad(wi, LANES), pad(dt, LANES)]
    return jnp.concatenate(cols, axis=1).astype(BF16)


def kernel(x_prompt, x_sample, cache_k, cache_v, cache_idx_k, state_ssm, state_ssm_conv, state_sconv, page_table,
           c_prompt, c_sample, ada_w, ada_b, ln_g, ln_b, even_w_in, even_w_out, ssm_conv_w, ssm_conv_b,
           ssm_dt_bias, ssm_a_log, ssm_d, ssm_norm_w, odd_w_in, odd_w_out, sconv_w, gmlp_ln_g, gmlp_ln_b,
           gmlp_ws, gmlp_bs, router_w, router_b, moe_w_gu, moe_b_gu, moe_w_down, moe_b_down):
    bp, seq, d = x_prompt.shape
    bs, dseq, _ = x_sample.shape
    depth = ada_w.shape[0]
    n_mod = 16
    c_all = jnp.zeros((n_mod, d), F32).at[:bp].set(c_prompt).at[bp:bp + bs].set(c_sample)
    m_all = adaln_all(c_all, ada_w, ada_b)
    mods = [[Mod(m_all[l, :bp], seq, False), Mod(m_all[l, bp:bp + bs], dseq, True)] for l in range(depth)]
    groups = [(bp, seq), (bs, dseq)]
    xs = [x_prompt.reshape(bp * seq, d), x_sample.reshape(bs * dseq, d)]
    hs = [modulate(xs[g], mods[0][g]) for g in range(2)]
    page_table = page_table.astype(I32)

    kp, vp, ikp, ks_, vs_, iks = [], [], [], [], [], []
    ssm_o = [[], []]
    ssmconv_o = [[], []]
    sconv_o = [[], []]
    gmv_o = [[], []]
    for layer in range(depth):
        j = layer // 2
        ya, yb = [None, None], [None, None]
        if layer % 2 == 0:
            w_in = _even_w_in(even_w_in[j])
            mamba_w = (ssm_conv_w[j], ssm_conv_b[j], ssm_dt_bias[j], ssm_a_log[j], ssm_d[j], ssm_norm_w[j])
            for g, (nb, sq) in enumerate(groups):
                proj = matmul(hs[g], w_in, EV_TN)
                p3 = proj.reshape(nb, sq, EV_N)
                k_new = p3[:, :, EV_K:EV_K + A_KV]
                v_new = p3[:, :, EV_V:EV_V + A_KV]
                ki_new = p3[:, :, EV_KK:EV_KK + IDX_DIM]
                xbc_tail = jnp.concatenate([p3[:, sq - (SSM_CONV - 1):, EV_XS:EV_XS + SSM_D],
                                            p3[:, sq - (SSM_CONV - 1):, EV_BC:EV_BC + BC_DIM]], axis=-1)
                if g == 0:
                    ya[g] = dsa_prompt(proj, nb, sq)
                    conv_prev = jnp.zeros((nb, SSM_CONV - 1, SSM_D + BC_DIM), F32)
                    h0 = jnp.zeros((nb, SSM_HEADS, SSM_HEAD_DIM, SSM_STATE), F32)
                    kp.append(k_new.reshape(nb, sq, ATT_KV_HEADS, HEAD_DIM))
                    vp.append(v_new.reshape(nb, sq, ATT_KV_HEADS, HEAD_DIM))
                    ikp.append(ki_new)
                else:
                    qs_rows = p3[:, :, EV_QI:EV_QI + IDX_HEADS * IDX_DIM].reshape(nb, sq * IDX_HEADS, IDX_DIM)
                    wcol = p3[:, :, EV_WI:EV_WI + IDX_HEADS].reshape(nb, sq * IDX_HEADS, 1)
                    scores = dsa_sample_scores(page_table, qs_rows, wcol, cache_idx_k, j)
                    padr = lambda t: jnp.concatenate(
                        [t, jnp.zeros((nb, LANES - sq, t.shape[2]), t.dtype)], axis=1)
                    att = dsa_sample_attend(page_table, p3[:, :, EV_Q:EV_Q + D_MODEL], qs_rows, wcol, padr(ki_new),
                                            scores, padr(k_new), padr(v_new), cache_k, cache_v, j)
                    ya[g] = att.reshape(nb * sq, D_MODEL)
                    conv_prev = state_ssm_conv[j]
                    h0 = state_ssm[j]
                    ks_.append(k_new.reshape(nb, sq, ATT_KV_HEADS, HEAD_DIM))
                    vs_.append(v_new.reshape(nb, sq, ATT_KV_HEADS, HEAD_DIM))
                    iks.append(ki_new)
                yb[g], h_last = ssd_mixer(proj, nb, sq, conv_prev, h0, *mamba_w)
                ssm_o[g].append(h_last)
                ssmconv_o[g].append(xbc_tail)
            w_out = even_w_out[j]
        else:
            w_in = odd_w_in[j].astype(BF16)
            for g, (nb, sq) in enumerate(groups):
                proj = matmul(hs[g], w_in, OD_TN)
                prev = jnp.zeros((nb, SC_WIDTH - 1, D_MODEL), F32) if g == 0 else state_sconv[j]
                ya[g], yb[g], sc_state, gm_v = odd_mixer(proj, nb, sq, prev, sconv_w[j], gmlp_ln_g[j], gmlp_ln_b[j],
                                                         gmlp_ws[j], gmlp_bs[j])
                sconv_o[g].append(sc_state)
                gmv_o[g].append(gm_v)
            w_out = odd_w_out[j]
        op_args = (ln_g[layer, 0], ln_b[layer, 0], router_w[layer], router_b[layer])
        x1s, hs_s, tes, tgs = outproj_ln_router(ya[1], yb[1], w_out, xs[1], mods[layer][1], *op_args)
        tail = jnp.zeros((OUTPROJ_TM, d), F32).at[:hs_s.shape[0]].set(hs_s)
        x1p, h_all, tep, tgp = outproj_ln_router(ya[0], yb[0], w_out, xs[0], mods[layer][0], *op_args, tail=tail)
        next_mods = mods[layer + 1] if layer + 1 < depth else None
        xs, hs = moe_sublayer([x1p, x1s], h_all, [tep, tes], [tgp, tgs], mods[layer], next_mods,
                              ln_g[layer, 1], ln_b[layer, 1], moe_w_gu, moe_b_gu, moe_w_down, moe_b_down, layer)
    return (xs[0].reshape(bp, seq, d), xs[1].reshape(bs, dseq, d),
            jnp.stack(kp), jnp.stack(vp), jnp.stack(ikp),
            jnp.stack(ks_), jnp.stack(vs_), jnp.stack(iks),
            jnp.stack(ssm_o[0]), jnp.stack(ssm_o[1]),
            jnp.stack(ssmconv_o[0]), jnp.stack(ssmconv_o[1]),
            jnp.stack(sconv_o[0]), jnp.stack(sconv_o[1]),
            jnp.stack(gmv_o[0]), jnp.stack(gmv_o[1]))
```

```python
import functools
import math

import jax
import jax.numpy as jnp
from jax import lax
from jax.experimental import pallas as pl
from jax.experimental.pallas import tpu as pltpu

F32 = jnp.float32
BF16 = jnp.bfloat16
I32 = jnp.int32

D_MODEL = 2048
DEPTH = 4
PAGE_SIZE = 128
HEAD_DIM = 128
ATT_HEADS = D_MODEL // HEAD_DIM
ATT_KV_HEADS = ATT_HEADS // 4
KV_REP = ATT_HEADS // ATT_KV_HEADS
IDX_HEADS = 16
IDX_DIM = 64
TOPK_MAX = 256
SSM_HEAD_DIM = 64
SSM_D = D_MODEL
SSM_HEADS = SSM_D // SSM_HEAD_DIM
SSM_GROUPS = 4
SSM_STATE = 128
SSM_CONV = 4
SSM_CHUNK = 128
SC_WIDTH = 3
GM_GROUPS = 16
GM_CHUNK = 128
N_EXPERTS = 32
TOP_K = 4
SWIGLU_LIMIT = 7.0
SWIGLU_ALPHA = 1.702
DN_ALPHA = (2 * DEPTH) ** 0.25
LN_EPS = 1e-5
RMS_EPS = 1e-5
A_KV = ATT_KV_HEADS * HEAD_DIM
BC_DIM = 2 * SSM_GROUPS * SSM_STATE
GROUP_COLS = SSM_D // SSM_GROUPS

LANES = 128
SUBLANES = 8
VMEM_LIMIT = 56 * 1024 * 1024

EV_Q, EV_Z, EV_XS, EV_QI, EV_BC, EV_K, EV_V, EV_KK, EV_WI, EV_DT, EV_N = (
    0, 2048, 4096, 6144, 7168, 8192, 8704, 9216, 9344, 9472, 9600)
EV_TN = 1920
OD_TN = 2048
DSA_SEG_BLOCKS = 2
OUTPROJ_TM = 256
MOE_TM = 1024
MOE_Q = 256
MOE_TN = 512
MOE_TN_DOWN = 1024
NEG_BIG = -1e30
INT_MIN = -2147483648


def _cparams(sem, **kw):
    return pltpu.CompilerParams(dimension_semantics=sem, vmem_limit_bytes=VMEM_LIMIT, **kw)


def _sds(shape, dtype=F32):
    return jax.ShapeDtypeStruct(shape, dtype)


def _silu(x):
    return x * jax.nn.sigmoid(x)


def _nt_dot(a, b):
    return lax.dot_general(a, b, (((1,), (1,)), ((), ())), preferred_element_type=F32)


def _dot(a, b):
    return jnp.dot(a, b, preferred_element_type=F32)


def _split3(x):
    hi = x.astype(BF16)
    r1 = x - hi.astype(F32)
    mid = r1.astype(BF16)
    lo = (r1 - mid.astype(F32)).astype(BF16)
    return hi, mid, lo


def _dot_exact_rhs(x, m_bf16):
    hi, mid, lo = _split3(x)
    return _dot(hi, m_bf16) + _dot(mid, m_bf16) + _dot(lo, m_bf16)


def _dot_exact_lhs(m_bf16, x):
    hi, mid, lo = _split3(x)
    return _dot(m_bf16, hi) + _dot(m_bf16, mid) + _dot(m_bf16, lo)


def _sort_key(x):
    b = pltpu.bitcast(x + 0.0, I32)
    return b ^ ((b >> 31) & jnp.int32(0x7FFFFFFF))


def _kth_largest_key(keys, k):
    rows = keys[0].shape[0]

    def body(it, p):
        cand = p | (jnp.int32(1) << (jnp.int32(31) - it))
        t = cand ^ jnp.int32(INT_MIN)
        cnt = jnp.zeros((rows, 1), I32)
        for kk in keys:
            cnt = cnt + jnp.sum((kk >= t).astype(I32), axis=-1, keepdims=True)
        return jnp.where(cnt >= k, cand, p)

    p = lax.fori_loop(0, 32, body, jnp.zeros((rows, 1), I32))
    return p ^ jnp.int32(INT_MIN)


def _adaln_kernel(c_ref, w_ref, b_ref, o_ref):
    s = _silu(c_ref[...]).astype(BF16)
    o_ref[...] = _dot(s, w_ref[...].astype(BF16)) + b_ref[...]


def adaln_all(c_all, ada_w, ada_b):
    depth, d, n = ada_w.shape
    rows = c_all.shape[0]
    tn = 1536
    return pl.pallas_call(
        _adaln_kernel,
        out_shape=_sds((depth, rows, n)),
        grid=(depth, n // tn),
        in_specs=[pl.BlockSpec((rows, d), lambda l, j: (0, 0)),
                  pl.BlockSpec((None, d, tn), lambda l, j: (l, 0, j)),
                  pl.BlockSpec((None, 1, tn), lambda l, j: (l, 0, j))],
        out_specs=pl.BlockSpec((None, rows, tn), lambda l, j: (l, 0, j)),
        compiler_params=_cparams(("arbitrary", "arbitrary")),
        name="adaln",
    )(c_all, ada_w, ada_b.reshape(depth, 1, n))


class Mod:
    def __init__(self, per_batch, rows_per_batch, expand):
        nb = per_batch.shape[0]
        self.rows_per_batch = rows_per_batch
        self.expand = expand
        if expand:
            m = per_batch.reshape(nb, 6, D_MODEL)
            self.rows = [jnp.repeat(m[:, w], rows_per_batch, axis=0) for w in range(6)]
        else:
            self.table = per_batch.reshape(nb * 6, 1, D_MODEL)

    def arg(self, which):
        return self.rows[which] if self.expand else self.table

    def spec(self, which, tm):
        if self.expand:
            return pl.BlockSpec((tm, D_MODEL), lambda i: (i, 0))
        rpb = self.rows_per_batch
        return pl.BlockSpec((None, 1, D_MODEL), lambda i: ((i * tm) // rpb * 6 + which, 0, 0))


def _row_tile(m, target):
    return min(m, target)


def _modulate_kernel(x_ref, sh_ref, sc_ref, o_ref):
    o_ref[...] = (x_ref[...] * (1.0 + sc_ref[...]) + sh_ref[...]).astype(o_ref.dtype)


def modulate(x, mod, out_dtype=BF16):
    m, d = x.shape
    tm = _row_tile(m, 512)
    return pl.pallas_call(
        _modulate_kernel,
        out_shape=_sds((m, d), out_dtype),
        grid=(m // tm,),
        in_specs=[pl.BlockSpec((tm, d), lambda i: (i, 0)), mod.spec(0, tm), mod.spec(1, tm)],
        out_specs=pl.BlockSpec((tm, d), lambda i: (i, 0)),
        compiler_params=_cparams(("arbitrary",)),
        name="modulate",
    )(x, mod.arg(0), mod.arg(1))


def _matmul_kernel(x_ref, w_ref, o_ref):
    o_ref[...] = _dot(x_ref[...], w_ref[...]).astype(o_ref.dtype)


def matmul(x, w, tn, out_dtype=F32):
    m, k = x.shape
    n = w.shape[1]
    tm = _row_tile(m, 512)
    return pl.pallas_call(
        _matmul_kernel,
        out_shape=_sds((m, n), out_dtype),
        grid=(n // tn, m // tm),
        in_specs=[pl.BlockSpec((tm, k), lambda j, i: (i, 0)),
                  pl.BlockSpec((k, tn), lambda j, i: (0, j))],
        out_specs=pl.BlockSpec((tm, tn), lambda j, i: (i, j)),
        compiler_params=_cparams(("arbitrary", "arbitrary")),
        name="in_proj",
    )(x, w)


def _layer_norm_rows(z, g, b):
    mu = jnp.mean(z, axis=-1, keepdims=True)
    zc = z - mu
    var = jnp.mean(zc * zc, axis=-1, keepdims=True)
    return zc * lax.rsqrt(var + LN_EPS) * g + b


def _top4_softmax(logits):
    tm = logits.shape[0]
    lane = lax.broadcasted_iota(I32, (tm, LANES), 1)
    v = logits
    vals, idxs = [], []
    for _ in range(TOP_K):
        m = jnp.max(v, axis=-1, keepdims=True)
        idx = jnp.min(jnp.where(v == m, lane, LANES), axis=-1, keepdims=True)
        vals.append(m)
        idxs.append(idx)
        v = jnp.where(lane == idx, -jnp.inf, v)
    es = [jnp.exp(m - vals[0]) for m in vals]
    den = es[0] + es[1] + es[2] + es[3]
    e_out = jnp.zeros((tm, LANES), I32)
    g_out = jnp.zeros((tm, LANES), F32)
    for k in range(TOP_K):
        e_out = jnp.where(lane == k, idxs[k], e_out)
        g_out = jnp.where(lane == k, es[k] / den, g_out)
    return e_out, g_out


def _outproj_kernel(ya_ref, yb_ref, wa_ref, wb_ref, x_ref, gate_ref, lng_ref, lnb_ref, sh_ref, sc_ref,
                    rw_ref, rb_ref, *rest, n_tiles, has_tail):
    if has_tail:
        tail_ref, xo_ref, h_ref, te_ref, tg_ref = rest
    else:
        xo_ref, h_ref, te_ref, tg_ref = rest

    def body():
        f = _dot(ya_ref[...], wa_ref[...]) + _dot(yb_ref[...], wb_ref[...])
        z = DN_ALPHA * x_ref[...] + (1.0 + gate_ref[...]) * f
        xn = _layer_norm_rows(z, lng_ref[...], lnb_ref[...])
        xo_ref[...] = xn
        h = xn * (1.0 + sc_ref[...]) + sh_ref[...]
        h_ref[...] = h
        logits = _dot(h.astype(BF16), rw_ref[...]) + rb_ref[...]
        e_out, g_out = _top4_softmax(logits)
        te_ref[...] = e_out
        tg_ref[...] = g_out

    if has_tail:
        i = pl.program_id(0)
        pl.when(i < n_tiles)(body)

        @pl.when(i == n_tiles)
        def _():
            h_ref[...] = tail_ref[...]
    else:
        body()


def outproj_ln_router(ya, yb, w_out, x, mod, ln_g, ln_b, router_w, router_b, tail=None):
    m, d = x.shape
    ka = ya.shape[1]
    tm = _row_tile(m, OUTPROJ_TM)
    n_tiles = m // tm
    has_tail = tail is not None
    wa = w_out[:ka].astype(BF16)
    wb = w_out[ka:].astype(BF16)
    rw = jnp.zeros((d, LANES), F32).at[:, :N_EXPERTS].set(router_w)
    rw = rw.astype(BF16)
    rb = jnp.full((1, LANES), NEG_BIG, F32).at[0, :N_EXPERTS].set(router_b)
    last = n_tiles - 1
    clamp = lambda spec: pl.BlockSpec(spec.block_shape, (lambda f: (lambda i: f(jnp.minimum(i, last))))(spec.index_map))
    const = lambda shape: pl.BlockSpec(shape, lambda i: (0,) * len(shape), pipeline_mode=pl.Buffered(1))
    row = lambda w: clamp(pl.BlockSpec((tm, w), lambda i: (i, 0)))
    in_specs = [row(ka), row(yb.shape[1]), const(wa.shape), const(wb.shape), row(d),
                clamp(mod.spec(2, tm)), const((1, d)), const((1, d)), clamp(mod.spec(3, tm)), clamp(mod.spec(4, tm)),
                const((d, LANES)), const((1, LANES))]
    args = [ya, yb, wa, wb, x, mod.arg(2), ln_g.reshape(1, d), ln_b.reshape(1, d), mod.arg(3), mod.arg(4),
            rw, rb]
    if has_tail:
        in_specs.append(const((tm, d)))
        args.append(tail)
    h_rows = m + (tm if has_tail else 0)
    return pl.pallas_call(
        functools.partial(_outproj_kernel, n_tiles=n_tiles, has_tail=has_tail),
        out_shape=(_sds((m, d)), _sds((h_rows, d)), _sds((m, LANES), I32), _sds((m, LANES))),
        grid=(n_tiles + (1 if has_tail else 0),),
        in_specs=in_specs,
        out_specs=(row(d), pl.BlockSpec((tm, d), lambda i: (i, 0)), row(LANES), row(LANES)),
        compiler_params=_cparams(("arbitrary",)),
        name="out_proj_ln_router",
    )(*args)


def _index_scores_block(qi_ref, wi_ref, kk):
    tq = qi_ref.shape[0]
    lane = lax.broadcasted_iota(I32, (tq, LANES), 1)
    wi = (wi_ref[...] * (IDX_HEADS ** -0.5)).astype(BF16).astype(F32)
    scores = None
    for p in range(IDX_HEADS // 2):
        qp = qi_ref[:, p * LANES:(p + 1) * LANES]
        for half in range(2):
            sel = (lane < IDX_DIM) if half == 0 else (lane >= IDX_DIM)
            qh = jnp.where(sel, qp, 0.0).astype(BF16)
            d = _nt_dot(qh, kk) * (IDX_DIM ** -0.5)
            h = 2 * p + half
            term = wi[:, h:h + 1] * jnp.maximum(d, 0.0).astype(BF16).astype(F32)
            scores = term if scores is None else scores + term
    return scores


def _dsa_prompt_kernel(q_ref, qi_ref, wi_ref, kk_ref, k_ref, v_ref, o_ref, *, n_sel, q0):
    tq = q_ref.shape[0]
    L = k_ref.shape[0]
    qb = pl.program_id(1)
    pos = q0 + qb * tq + lax.broadcasted_iota(I32, (tq, L), 0)
    kpos = lax.broadcasted_iota(I32, (tq, L), 1)
    valid = kpos <= pos
    if n_sel >= L:
        sel = valid
    else:
        scores = _index_scores_block(qi_ref, wi_ref, kk_ref[...].astype(BF16))
        key = jnp.where(valid, _sort_key(scores), _sort_key(jnp.full((1, 1), -jnp.inf, F32)))
        thr = _kth_largest_key([key], n_sel)
        sel = jnp.logical_and(key >= thr, valid)
    for g in range(ATT_KV_HEADS):
        kg = k_ref[:, g * HEAD_DIM:(g + 1) * HEAD_DIM].astype(BF16)
        vg = v_ref[:, g * HEAD_DIM:(g + 1) * HEAD_DIM].astype(BF16)
        for r in range(KV_REP):
            c0 = (g * KV_REP + r) * HEAD_DIM
            qh = q_ref[:, c0:c0 + HEAD_DIM].astype(BF16)
            logits = jnp.where(sel, _nt_dot(qh, kg) * (HEAD_DIM ** -0.5), NEG_BIG)
            m = jnp.max(logits, axis=-1, keepdims=True)
            p = jnp.exp(logits - m)
            l = jnp.sum(p, axis=-1, keepdims=True)
            o = _dot((p * (1.0 / l)).astype(BF16), vg)
            o_ref[:, c0:c0 + HEAD_DIM] = o.astype(o_ref.dtype)


def dsa_prompt(proj, n_batch, seq):
    tq = 128
    n_sel = min(TOPK_MAX, seq // 4)
    seg_blocks = min(DSA_SEG_BLOCKS, seq // tq)
    p3 = proj.reshape(n_batch, seq, proj.shape[1])
    outs = []
    for s in range(seq // (seg_blocks * tq)):
        q0 = s * seg_blocks * tq
        lc = q0 + seg_blocks * tq
        qrow = lambda w, col: pl.BlockSpec((None, tq, w), lambda b, i, s=s, col=col: (b, s * seg_blocks + i, col))
        keys = lambda w, col: pl.BlockSpec((None, lc, w), lambda b, i, col=col: (b, 0, col))
        outs.append(pl.pallas_call(
            functools.partial(_dsa_prompt_kernel, n_sel=n_sel, q0=q0),
            out_shape=_sds((n_batch, seg_blocks * tq, D_MODEL), BF16),
            grid=(n_batch, seg_blocks),
            in_specs=[qrow(D_MODEL, EV_Q // D_MODEL), qrow(1024, EV_QI // 1024), qrow(LANES, EV_WI // LANES),
                      keys(LANES, EV_KK // LANES), keys(A_KV, EV_K // A_KV), keys(A_KV, EV_V // A_KV)],
            out_specs=pl.BlockSpec((None, tq, D_MODEL), lambda b, i: (b, i, 0)),
            compiler_params=_cparams(("arbitrary", "arbitrary")),
            name="dsa_prompt",
        )(p3, p3, p3, p3, p3, p3))
    return jnp.concatenate(outs, axis=1).reshape(n_batch * seq, D_MODEL)


S1_PAGES = 16
S3_PAGES = 8


def _rows_index_scores(qs, wcol, keys_bf16):
    d = _nt_dot(qs, keys_bf16) * (IDX_DIM ** -0.5)
    w = wcol.astype(BF16).astype(F32) * jnp.maximum(d, 0.0).astype(BF16).astype(F32)
    nq = qs.shape[0] // IDX_HEADS
    return jnp.sum(w.reshape(nq, IDX_HEADS, w.shape[-1]), axis=1)


def _dsa_s1_kernel(pt_ref, qs_ref, wcol_ref, *refs):
    ik_refs, o_ref = refs[:S1_PAGES], refs[S1_PAGES]
    qs = qs_ref[...].astype(BF16)
    wcol = wcol_ref[...] * (IDX_HEADS ** -0.5)
    for i in range(S1_PAGES):
        o_ref[:, i * PAGE_SIZE:(i + 1) * PAGE_SIZE] = _rows_index_scores(qs, wcol, ik_refs[i][...].astype(BF16))


def dsa_sample_scores(page_table, qs_rows, wcol, pool_ik, layer_j):
    nb, n_pages = page_table.shape
    nq = qs_rows.shape[1] // IDX_HEADS
    steps = n_pages // S1_PAGES

    def ik_spec(i):
        return pl.BlockSpec((None, None, PAGE_SIZE, IDX_DIM),
                            lambda b, c, pt: (layer_j, pt[b, c * S1_PAGES + i], 0, 0))

    return pl.pallas_call(
        _dsa_s1_kernel,
        out_shape=_sds((nb, nq, n_pages * PAGE_SIZE)),
        grid_spec=pltpu.PrefetchScalarGridSpec(
            num_scalar_prefetch=1, grid=(nb, steps),
            in_specs=[pl.BlockSpec((None, nq * IDX_HEADS, IDX_DIM), lambda b, c, pt: (b, 0, 0)),
                      pl.BlockSpec((None, nq * IDX_HEADS, 1), lambda b, c, pt: (b, 0, 0))]
            + [ik_spec(i) for i in range(S1_PAGES)],
            out_specs=pl.BlockSpec((None, nq, S1_PAGES * PAGE_SIZE), lambda b, c, pt: (b, 0, c))),
        compiler_params=_cparams(("arbitrary", "arbitrary")),
        name="dsa_sample_scores",
    )(page_table, qs_rows, wcol, *([pool_ik] * S1_PAGES))


def _dsa_s3_kernel(pt_ref, q_ref, qs_ref, wcol_ref, kinew_ref, sall_ref, sblk_ref, knew_ref, vnew_ref, *refs,
                   n_sel, n_steps):
    kp = refs[:S3_PAGES]
    vp = refs[S3_PAGES:2 * S3_PAGES]
    o_ref = refs[2 * S3_PAGES]
    thr_s, newkey_s, m_s, l_s, acc_s = refs[2 * S3_PAGES + 1:]
    c = pl.program_id(1)
    nq = q_ref.shape[0]
    rows = KV_REP * nq
    neg_inf_key = _sort_key(jnp.full((1, 1), -jnp.inf, F32))

    def new_valid():
        t = lax.broadcasted_iota(I32, (nq, LANES), 0)
        s = lax.broadcasted_iota(I32, (nq, LANES), 1)
        return jnp.logical_and(s <= t, s < nq)

    @pl.when(c == 0)
    def _():
        wcol = wcol_ref[...] * (IDX_HEADS ** -0.5)
        s_new = _rows_index_scores(qs_ref[...].astype(BF16), wcol, kinew_ref[...].astype(BF16))
        nk = jnp.where(new_valid(), _sort_key(s_new), neg_inf_key)
        newkey_s[...] = nk
        thr_s[...] = _kth_largest_key([_sort_key(sall_ref[...]), nk], n_sel)
        m_s[...] = jnp.full(m_s.shape, NEG_BIG, F32)
        l_s[...] = jnp.zeros(l_s.shape, F32)
        acc_s[...] = jnp.zeros(acc_s.shape, F32)

    thr = thr_s[...]
    qg = [jnp.concatenate([q_ref[:, (g * KV_REP + r) * HEAD_DIM:(g * KV_REP + r + 1) * HEAD_DIM]
                           for r in range(KV_REP)], axis=0).astype(BF16) for g in range(ATT_KV_HEADS)]

    def update(g, kg, vg, sel_rows):
        logits = _nt_dot(qg[g], kg) * (HEAD_DIM ** -0.5)
        lm = jnp.where(sel_rows, logits, NEG_BIG)
        m_old = m_s[g]
        m_new = jnp.maximum(m_old, jnp.max(lm, axis=-1, keepdims=True))
        p = jnp.where(sel_rows, jnp.exp(lm - m_new), 0.0)
        a = jnp.exp(m_old - m_new)
        l_s[g] = a * l_s[g] + jnp.sum(p, axis=-1, keepdims=True)
        acc_s[g] = a * acc_s[g] + _dot(p.astype(BF16), vg)
        m_s[g] = m_new

    def tile_rows(sel):
        f = jnp.where(sel, 1.0, 0.0)
        return jnp.concatenate([f] * KV_REP, axis=0) > 0.5

    def head_rows(page_refs, g):
        return jnp.concatenate([r[pl.ds(g, PAGE_SIZE, stride=ATT_KV_HEADS), :] for r in page_refs],
                               axis=0).astype(BF16)

    sel = tile_rows(_sort_key(sblk_ref[...]) >= thr)
    for g in range(ATT_KV_HEADS):
        update(g, head_rows(kp, g), head_rows(vp, g), sel)

    @pl.when(c == n_steps - 1)
    def _():
        sel = tile_rows(jnp.logical_and(newkey_s[...] >= thr, new_valid()))
        for g in range(ATT_KV_HEADS):
            update(g, knew_ref[:, g * HEAD_DIM:(g + 1) * HEAD_DIM].astype(BF16),
                   vnew_ref[:, g * HEAD_DIM:(g + 1) * HEAD_DIM].astype(BF16), sel)
        for g in range(ATT_KV_HEADS):
            o = acc_s[g] / l_s[g]
            for r in range(KV_REP):
                c0 = (g * KV_REP + r) * HEAD_DIM
                o_ref[:, c0:c0 + HEAD_DIM] = o[r * nq:(r + 1) * nq].astype(o_ref.dtype)


def dsa_sample_attend(page_table, q, qs_rows, wcol, ki_new, scores, k_new, v_new, pool_k, pool_v, layer_j):
    nb, n_pages = page_table.shape
    nq = q.shape[1]
    past = n_pages * PAGE_SIZE
    n_sel = min(TOPK_MAX, (past + nq) // 4)
    n_steps = n_pages // S3_PAGES
    rows = KV_REP * nq

    def page_spec(i):
        return pl.BlockSpec((None, None, PAGE_SIZE * ATT_KV_HEADS, HEAD_DIM),
                            lambda b, c, pt: (layer_j, pt[b, c * S3_PAGES + i], 0, 0))

    pool_k = pool_k.reshape(pool_k.shape[0], pool_k.shape[1], PAGE_SIZE * ATT_KV_HEADS, HEAD_DIM)
    pool_v = pool_v.reshape(pool_v.shape[0], pool_v.shape[1], PAGE_SIZE * ATT_KV_HEADS, HEAD_DIM)
    per_b = lambda *shape: pl.BlockSpec((None,) + shape, lambda b, c, pt: (b,) + (0,) * len(shape))
    return pl.pallas_call(
        functools.partial(_dsa_s3_kernel, n_sel=n_sel, n_steps=n_steps),
        out_shape=_sds((nb, nq, D_MODEL), BF16),
        grid_spec=pltpu.PrefetchScalarGridSpec(
            num_scalar_prefetch=1, grid=(nb, n_steps),
            in_specs=[per_b(nq, D_MODEL), per_b(nq * IDX_HEADS, IDX_DIM), per_b(nq * IDX_HEADS, 1),
                      per_b(LANES, IDX_DIM), per_b(nq, past),
                      pl.BlockSpec((None, nq, S3_PAGES * PAGE_SIZE), lambda b, c, pt: (b, 0, c)),
                      per_b(LANES, A_KV), per_b(LANES, A_KV)]
            + [page_spec(i) for i in range(S3_PAGES)] * 2,
            out_specs=per_b(nq, D_MODEL),
            scratch_shapes=[pltpu.VMEM((nq, 1), I32), pltpu.VMEM((nq, LANES), I32),
                            pltpu.VMEM((ATT_KV_HEADS, rows, 1), F32), pltpu.VMEM((ATT_KV_HEADS, rows, 1), F32),
                            pltpu.VMEM((ATT_KV_HEADS, rows, HEAD_DIM), F32)]),
        compiler_params=_cparams(("arbitrary", "arbitrary")),
        name="dsa_sample_attend",
    )(page_table, q, qs_rows, wcol, ki_new, scores, scores, k_new, v_new,
      *([pool_k] * S3_PAGES), *([pool_v] * S3_PAGES))


def _conv_rows(buf, w_ref, rows, width):
    base = SUBLANES - (width - 1)
    out = None
    for j in range(width):
        term = w_ref[j:j + 1, :] * buf[base + j:base + j + rows, :]
        out = term if out is None else out + term
    return out


def _ssd_kernel(z_ref, xs_ref, bc_ref, dt_ref, prevx_ref, prevb_ref, h0_ref,
                cwx_ref, cwb_ref, cbx_ref, cbb_ref, dtb_ref, alog_ref, dskip_ref, nw_ref, e_ref,
                y_ref, hl_ref, bufx, bufb, xs_s, bc_s, st_s, *, rows, n_chunks):
    c = pl.program_id(1)
    ck = SSM_CHUNK

    @pl.when(c == 0)
    def _():
        bufx[0:SUBLANES] = prevx_ref[...]
        bufb[0:SUBLANES] = prevb_ref[...]
        st_s[...] = h0_ref[...]

    @pl.when(c > 0)
    def _():
        bufx[0:SUBLANES] = bufx[rows:rows + SUBLANES]
        bufb[0:SUBLANES] = bufb[rows:rows + SUBLANES]

    bufx[SUBLANES:SUBLANES + rows] = xs_ref[...]
    bufb[SUBLANES:SUBLANES + rows] = bc_ref[...]
    xc = _silu(_conv_rows(bufx, cwx_ref, rows, SSM_CONV) + cbx_ref[...])
    bcc = _silu(_conv_rows(bufb, cwb_ref, rows, SSM_CONV) + cbb_ref[...])
    if rows < ck:
        xs_s[...] = jnp.zeros(xs_s.shape, F32)
        bc_s[...] = jnp.zeros(bc_s.shape, F32)
    xs_s[0:rows] = xc
    bc_s[0:rows] = bcc

    dt_raw = dt_ref[...] + dtb_ref[...]
    dt = jnp.maximum(dt_raw, 0.0) + jnp.log1p(jnp.exp(-jnp.abs(dt_raw)))
    if rows < ck:
        dt = jnp.concatenate([dt, jnp.zeros((ck - rows, LANES), F32)], axis=0)
    a = -jnp.exp(alog_ref[...])
    ri = lax.broadcasted_iota(I32, (ck, ck), 0)
    ci = lax.broadcasted_iota(I32, (ck, ck), 1)
    tril = ri >= ci
    tril_b = jnp.where(tril, 1.0, 0.0).astype(BF16)
    acum = _dot_exact_lhs(tril_b, dt * a)
    acum_t = acum.T
    e = e_ref[...]
    dtx = _dot_exact_rhs(dt, e)
    acx = _dot_exact_rhs(acum, e)
    eax = jnp.exp(acx)
    last = acx[ck - 1:ck, :]
    dendx = jnp.exp(last - acx)
    cdecx = jnp.exp(last)

    xs = xs_s[...]
    xdt = xs * dtx
    xdt_b = xdt.astype(BF16)
    xdd_b = (xdt * dendx).astype(BF16)
    lane = lax.broadcasted_iota(I32, (ck, LANES), 1)
    lo_half = lane < SSM_HEAD_DIM
    hpg = SSM_HEADS // SSM_GROUPS
    y_parts = []
    for g in range(SSM_GROUPS):
        bm = bc_s[:, g * SSM_STATE:(g + 1) * SSM_STATE]
        cm = bc_s[:, (SSM_GROUPS + g) * SSM_STATE:(SSM_GROUPS + g + 1) * SSM_STATE]
        bm_b = bm.astype(BF16)
        cm_b = cm.astype(BF16)
        cb = _nt_dot(cm_b, bm_b)
        g0 = g * GROUP_COLS
        st = st_s[g]
        y_off = _dot(cm_b, st.astype(BF16)) * eax[:, g0:g0 + GROUP_COLS]
        pair_out = []
        for pr in range(hpg // 2):
            res = []
            for half in range(2):
                h = g * hpg + 2 * pr + half
                seg = acum[:, h:h + 1] - acum_t[h:h + 1, :]
                decay = jnp.exp(jnp.where(tril, seg, -jnp.inf))
                sc = (cb * decay).astype(BF16)
                res.append(_dot(sc, xdt_b[:, g0 + pr * LANES:g0 + (pr + 1) * LANES]))
            pair_out.append(jnp.where(lo_half, res[0], res[1]))
        y_parts.append(jnp.concatenate(pair_out, axis=1) + y_off)
        new_st = _dot(bm.T.astype(BF16), xdd_b[:, g0:g0 + GROUP_COLS])
        st_s[g] = st * cdecx[:, g0:g0 + GROUP_COLS] + new_st

    zz = z_ref[...]
    nw = nw_ref[...]
    for g in range(SSM_GROUPS):
        g0 = g * GROUP_COLS
        y = y_parts[g][0:rows] + dskip_ref[:, g0:g0 + GROUP_COLS] * xs[0:rows, g0:g0 + GROUP_COLS]
        y = y * _silu(zz[:, g0:g0 + GROUP_COLS])
        y = y * lax.rsqrt(jnp.mean(y * y, axis=-1, keepdims=True) + RMS_EPS)
        y_ref[:, g0:g0 + GROUP_COLS] = (y * nw[:, g0:g0 + GROUP_COLS]).astype(y_ref.dtype)

    @pl.when(c == n_chunks - 1)
    def _():
        hl_ref[...] = st_s[...]


def _pad_prev(prev):
    b, w1, ch = prev.shape
    return jnp.concatenate([jnp.zeros((b, SUBLANES - w1, ch), prev.dtype), prev], axis=1)


def _pad_rows(w, rows=SUBLANES):
    return jnp.concatenate([w, jnp.zeros((rows - w.shape[0], w.shape[1]), w.dtype)], axis=0)


def _pad_lanes(v, n=LANES):
    return jnp.zeros((1, n), v.dtype).at[0, :v.shape[0]].set(v)


def ssd_mixer(proj, n_batch, seq, conv_prev, h0, conv_w, conv_b, dt_bias, a_log, d_skip, norm_w):
    rows = SSM_CHUNK if seq % SSM_CHUNK == 0 else seq
    n_chunks = seq // rows
    hpg = SSM_HEADS // SSM_GROUPS
    prev = _pad_prev(conv_prev)
    prevx, prevb = prev[:, :, :SSM_D], prev[:, :, SSM_D:]
    h0k = h0.reshape(n_batch, SSM_GROUPS, hpg, SSM_HEAD_DIM, SSM_STATE).transpose(0, 1, 4, 2, 3)
    h0k = h0k.reshape(n_batch, SSM_GROUPS, SSM_STATE, GROUP_COLS)
    cw = _pad_rows(conv_w)
    e = (jnp.arange(LANES)[:, None] == (jnp.arange(SSM_D)[None, :] // SSM_HEAD_DIM)).astype(BF16)
    nb = n_chunks
    row = lambda w, col: pl.BlockSpec((rows, w), lambda b, c: (b * nb + c, col))
    per_b = lambda *shape: pl.BlockSpec((None,) + shape, lambda b, c: (b,) + (0,) * len(shape))
    const = lambda *shape: pl.BlockSpec(shape, lambda b, c: (0,) * len(shape))
    y, hl = pl.pallas_call(
        functools.partial(_ssd_kernel, rows=rows, n_chunks=n_chunks),
        out_shape=(_sds((n_batch * seq, SSM_D), BF16),
                   _sds((n_batch, SSM_GROUPS, SSM_STATE, GROUP_COLS))),
        grid=(n_batch, n_chunks),
        in_specs=[row(SSM_D, EV_Z // SSM_D), row(SSM_D, EV_XS // SSM_D), row(BC_DIM, EV_BC // BC_DIM),
                  row(LANES, EV_DT // LANES),
                  per_b(SUBLANES, SSM_D), per_b(SUBLANES, BC_DIM), per_b(SSM_GROUPS, SSM_STATE, GROUP_COLS),
                  const(SUBLANES, SSM_D), const(SUBLANES, BC_DIM), const(1, SSM_D), const(1, BC_DIM),
                  const(1, LANES), const(1, LANES), const(1, SSM_D), const(1, SSM_D), const(LANES, SSM_D)],
        out_specs=(pl.BlockSpec((rows, SSM_D), lambda b, c: (b * nb + c, 0)),
                   per_b(SSM_GROUPS, SSM_STATE, GROUP_COLS)),
        scratch_shapes=[pltpu.VMEM((rows + SUBLANES, SSM_D), F32), pltpu.VMEM((rows + SUBLANES, BC_DIM), F32),
                        pltpu.VMEM((SSM_CHUNK, SSM_D), F32), pltpu.VMEM((SSM_CHUNK, BC_DIM), F32),
                        pltpu.VMEM((SSM_GROUPS, SSM_STATE, GROUP_COLS), F32)],
        compiler_params=_cparams(("arbitrary", "arbitrary")),
        name="ssd_mixer",
    )(proj, proj, proj, proj, prevx, prevb, h0k,
      cw[:, :SSM_D], cw[:, SSM_D:], conv_b[None, :SSM_D], conv_b[None, SSM_D:],
      _pad_lanes(dt_bias), _pad_lanes(a_log), jnp.repeat(d_skip, SSM_HEAD_DIM)[None, :], norm_w[None, :], e)
    h_last = hl.reshape(n_batch, SSM_GROUPS, SSM_STATE, hpg, SSM_HEAD_DIM).transpose(0, 1, 3, 4, 2)
    return y, h_last.reshape(n_batch, SSM_HEADS, SSM_HEAD_DIM, SSM_STATE)


def _odd_kernel(bg_ref, cg_ref, sv_ref, u_ref, v_ref, prev_ref, scw_ref, lng_ref, lnb_ref, ws_ref, bst_ref,
                sc_ref, gm_ref, scst_ref, gmv_ref, buf, vpad, *, rows, n_chunks):
    c = pl.program_id(1)
    ck = GM_CHUNK

    @pl.when(c == 0)
    def _():
        buf[0:SUBLANES] = prev_ref[...]

    @pl.when(c > 0)
    def _():
        buf[0:SUBLANES] = buf[rows:rows + SUBLANES]

    buf[SUBLANES:SUBLANES + rows] = cg_ref[...] * sv_ref[...]
    sc_ref[...] = (bg_ref[...] * _conv_rows(buf, scw_ref, rows, SC_WIDTH)).astype(sc_ref.dtype)

    def gelu(x):
        return 0.5 * x * (1.0 + lax.erf(x * (0.5 ** 0.5)))

    u = gelu(u_ref[...])
    v = _layer_norm_rows(gelu(v_ref[...]), lng_ref[...], lnb_ref[...])
    if rows < ck:
        vpad[...] = jnp.zeros(vpad.shape, F32)
    vpad[0:rows] = v
    ri = lax.broadcasted_iota(I32, (ck, ck), 0)
    ci = lax.broadcasted_iota(I32, (ck, ck), 1)
    tril = ri >= ci
    gd = D_MODEL // GM_GROUPS
    for g in range(GM_GROUPS):
        wm = jnp.where(tril, ws_ref[g], 0.0).astype(BF16)
        mixed = _dot(wm, vpad[:, g * gd:(g + 1) * gd].astype(BF16)) + bst_ref[:, g:g + 1]
        gm_ref[:, g * gd:(g + 1) * gd] = (u[:, g * gd:(g + 1) * gd] * mixed[0:rows]).astype(gm_ref.dtype)

    @pl.when(c == n_chunks - 1)
    def _():
        scst_ref[...] = buf[rows:rows + SUBLANES]
        gmv_ref[...] = v


def odd_mixer(proj, n_batch, seq, sconv_prev, sconv_w, gm_ln_g, gm_ln_b, gm_ws, gm_bs):
    d = D_MODEL
    rows = GM_CHUNK if seq % GM_CHUNK == 0 else seq
    n_chunks = seq // rows
    nb = n_chunks
    prev = _pad_prev(sconv_prev)
    bst = jnp.zeros((GM_CHUNK, LANES), F32).at[:, :GM_GROUPS].set(gm_bs.T)
    row = lambda col: pl.BlockSpec((rows, d), lambda b, c: (b * nb + c, col))
    per_b = lambda *shape: pl.BlockSpec((None,) + shape, lambda b, c: (b,) + (0,) * len(shape))
    const = lambda *shape: pl.BlockSpec(shape, lambda b, c: (0,) * len(shape))
    sc, gm, scst, gmv = pl.pallas_call(
        functools.partial(_odd_kernel, rows=rows, n_chunks=n_chunks),
        out_shape=(_sds((n_batch * seq, d), BF16), _sds((n_batch * seq, d), BF16),
                   _sds((n_batch, SUBLANES, d)), _sds((n_batch, rows, d))),
        grid=(n_batch, n_chunks),
        in_specs=[row(0), row(1), row(2), row(3), row(4), per_b(SUBLANES, d), const(SUBLANES, d),
                  const(1, d), const(1, d), const(GM_GROUPS, GM_CHUNK, GM_CHUNK), const(GM_CHUNK, LANES)],
        out_specs=(pl.BlockSpec((rows, d), lambda b, c: (b * nb + c, 0)),
                   pl.BlockSpec((rows, d), lambda b, c: (b * nb + c, 0)),
                   per_b(SUBLANES, d), per_b(rows, d)),
        scratch_shapes=[pltpu.VMEM((rows + SUBLANES, d), F32), pltpu.VMEM((GM_CHUNK, d), F32)],
        compiler_params=_cparams(("arbitrary", "arbitrary")),
        name="odd_mixer",
    )(proj, proj, proj, proj, proj, prev, _pad_rows(sconv_w), gm_ln_g[None, :], gm_ln_b[None, :], gm_ws, bst)
    return sc, gm, scst[:, SUBLANES - (SC_WIDTH - 1):], gmv


def _row_copy(src_hbm, row, dst, slot, sem):
    return pltpu.make_async_copy(src_hbm.at[pl.ds(row, 1), :], dst.at[pl.ds(slot, 1), :], sem)


DISPATCH_UNROLL = 8


def _dispatch_kernel(nsub_ref, src_ref, h_hbm, o_ref, buf, sem):
    i = pl.program_id(0)
    tq = MOE_Q
    ns = nsub_ref[i]

    def issue(t8, carry):
        for u in range(DISPATCH_UNROLL):
            t = t8 * DISPATCH_UNROLL + u
            _row_copy(h_hbm, src_ref[0, t], buf, t, sem).start()
        return carry

    lax.fori_loop(0, ns * (tq // DISPATCH_UNROLL), issue, 0)

    def wait_sub(q, carry):
        pltpu.make_async_copy(h_hbm.at[pl.ds(0, tq), :], buf.at[pl.ds(0, tq), :], sem).wait()
        return carry

    lax.fori_loop(0, ns, wait_sub, 0)
    for q in range(buf.shape[0] // tq):
        @pl.when(q < ns)
        def _(q=q):
            o_ref[q * tq:(q + 1) * tq] = buf[q * tq:(q + 1) * tq].astype(o_ref.dtype)

        @pl.when(q >= ns)
        def _(q=q):
            o_ref[q * tq:(q + 1) * tq] = jnp.zeros((tq, o_ref.shape[1]), o_ref.dtype)


def moe_dispatch(row_src, block_nsub, h_all):
    d = h_all.shape[1]
    tm = MOE_TM
    n_blocks = block_nsub.shape[0]
    return pl.pallas_call(
        _dispatch_kernel,
        out_shape=_sds((n_blocks * tm, d), BF16),
        grid_spec=pltpu.PrefetchScalarGridSpec(
            num_scalar_prefetch=1, grid=(n_blocks,),
            in_specs=[pl.BlockSpec((None, 1, tm), lambda i, ns: (i, 0, 0), memory_space=pltpu.SMEM),
                      pl.BlockSpec(memory_space=pl.ANY)],
            out_specs=pl.BlockSpec((tm, d), lambda i, ns: (i, 0)),
            scratch_shapes=[pltpu.VMEM((tm, d), F32), pltpu.SemaphoreType.DMA(())]),
        compiler_params=_cparams(("arbitrary",), disable_bounds_checks=True),
        name="moe_dispatch",
    )(block_nsub, row_src.reshape(n_blocks, 1, tm), h_all)


def _first_of_expert(be_ref, i):
    prev = be_ref[jnp.maximum(i - 1, 0)]
    return jnp.logical_or(i == 0, be_ref[i] != prev)


def _for_live_rows(nsub_ref, i, o_ref, compute):
    n_sub_max = o_ref.shape[0] // MOE_Q
    ns = nsub_ref[i]
    for q in range(1, n_sub_max + 1):
        @pl.when(ns == q)
        def _(q=q):
            rows = q * MOE_Q
            o_ref[0:rows] = compute(rows).astype(o_ref.dtype)
            if q < n_sub_max:
                o_ref[rows:] = jnp.zeros((o_ref.shape[0] - rows, o_ref.shape[1]), o_ref.dtype)

    @pl.when(ns == 0)
    def _():
        o_ref[...] = jnp.zeros(o_ref.shape, o_ref.dtype)


def _expert_gu_kernel(be_ref, nsub_ref, nl_ref, x_ref, wg_ref, wu_ref, bg_ref, bu_ref, o_ref, wg_s, wu_s):
    i = pl.program_id(1)

    @pl.when(_first_of_expert(be_ref, i))
    def _():
        wg_s[...] = wg_ref[...].astype(BF16)
        wu_s[...] = wu_ref[...].astype(BF16)

    def compute(rows):
        x = x_ref[0:rows]
        g = _dot(x, wg_s[...]) + bg_ref[...]
        u = _dot(x, wu_s[...]) + bu_ref[...]
        g = jnp.minimum(g, SWIGLU_LIMIT)
        u = jnp.clip(u, -SWIGLU_LIMIT, SWIGLU_LIMIT)
        return (u + 1.0) * (g * jax.nn.sigmoid(SWIGLU_ALPHA * g))

    _for_live_rows(nsub_ref, i, o_ref, compute)


def _expert_down_kernel(be_ref, nsub_ref, nl_ref, a_ref, w_ref, b_ref, o_ref, w_s):
    i = pl.program_id(1)

    @pl.when(_first_of_expert(be_ref, i))
    def _():
        w_s[...] = w_ref[...].astype(BF16)

    _for_live_rows(nsub_ref, i, o_ref, lambda rows: _dot(a_ref[0:rows], w_s[...]) + b_ref[...])


def moe_experts(xs, block_e, block_nsub, n_live, w_gu, b_gu, w_down, b_down, layer):
    n_rows, d = xs.shape
    tm, tn = MOE_TM, MOE_TN
    n_blocks = n_rows // tm
    de = w_down.shape[2]
    nj = de // tn
    live = lambda i, nl: jnp.minimum(i, nl[0] - 1)
    act = pl.pallas_call(
        _expert_gu_kernel,
        out_shape=_sds((n_rows, de), BF16),
        grid_spec=pltpu.PrefetchScalarGridSpec(
            num_scalar_prefetch=3, grid=(nj, n_blocks),
            in_specs=[pl.BlockSpec((tm, d), lambda j, i, be, ns, nl: (live(i, nl), 0)),
                      pl.BlockSpec((None, None, d, tn), lambda j, i, be, ns, nl: (layer, be[i], 0, j)),
                      pl.BlockSpec((None, None, d, tn), lambda j, i, be, ns, nl: (layer, be[i], 0, nj + j)),
                      pl.BlockSpec((None, None, 1, tn), lambda j, i, be, ns, nl: (layer, be[i], 0, j)),
                      pl.BlockSpec((None, None, 1, tn), lambda j, i, be, ns, nl: (layer, be[i], 0, nj + j))],
            out_specs=pl.BlockSpec((tm, tn), lambda j, i, be, ns, nl: (i, j)),
            scratch_shapes=[pltpu.VMEM((d, tn), BF16), pltpu.VMEM((d, tn), BF16)]),
        compiler_params=_cparams(("arbitrary", "arbitrary")),
        name="moe_expert_gu",
    )(block_e, block_nsub, n_live, xs, w_gu, w_gu, b_gu.reshape(b_gu.shape[0], b_gu.shape[1], 1, -1),
      b_gu.reshape(b_gu.shape[0], b_gu.shape[1], 1, -1))
    dn = w_down.shape[3]
    tn = MOE_TN_DOWN
    njd = dn // tn
    return pl.pallas_call(
        _expert_down_kernel,
        out_shape=_sds((n_rows, dn)),
        grid_spec=pltpu.PrefetchScalarGridSpec(
            num_scalar_prefetch=3, grid=(njd, n_blocks),
            in_specs=[pl.BlockSpec((tm, de), lambda j, i, be, ns, nl: (live(i, nl), 0)),
                      pl.BlockSpec((None, None, de, tn), lambda j, i, be, ns, nl: (layer, be[i], 0, j)),
                      pl.BlockSpec((None, None, 1, tn), lambda j, i, be, ns, nl: (layer, be[i], 0, j))],
            out_specs=pl.BlockSpec((tm, tn), lambda j, i, be, ns, nl: (i, j)),
            scratch_shapes=[pltpu.VMEM((de, tn), BF16)]),
        compiler_params=_cparams(("arbitrary", "arbitrary")),
        name="moe_expert_down",
    )(block_e, block_nsub, n_live, act, w_down, b_down.reshape(b_down.shape[0], b_down.shape[1], 1, -1))


def _combine_kernel(dest_ref, y_hbm, x_ref, tg_ref, gate_ref, lng_ref, lnb_ref, *rest, tok0, has_next):
    if has_next:
        sh_ref, sc_ref, xo_ref, ho_ref, buf, sem = rest
    else:
        xo_ref, buf, sem = rest
    i = pl.program_id(0)
    tm = x_ref.shape[0]

    def issue(t2, carry):
        for u in range(2):
            t = t2 * 2 + u
            base = (tok0 + i * tm + t) * TOP_K
            for k in range(TOP_K):
                _row_copy(y_hbm, dest_ref[base + k], buf.at[k], t, sem).start()
        return carry

    lax.fori_loop(0, tm // 2, issue, 0)
    for k in range(TOP_K):
        pltpu.make_async_copy(y_hbm.at[pl.ds(0, tm), :], buf.at[k], sem).wait()
    tg = tg_ref[...]
    f = tg[:, 0:1] * buf[0]
    for k in range(1, TOP_K):
        f = f + tg[:, k:k + 1] * buf[k]
    z = DN_ALPHA * x_ref[...] + (1.0 + gate_ref[...]) * f
    xn = _layer_norm_rows(z, lng_ref[...], lnb_ref[...])
    xo_ref[...] = xn
    if has_next:
        ho_ref[...] = (xn * (1.0 + sc_ref[...]) + sh_ref[...]).astype(ho_ref.dtype)


def moe_combine_ln(dest, y, x, top_g, mod, ln_g, ln_b, next_mod, tok0):
    m, d = x.shape
    tm = _row_tile(m, 256)
    has_next = next_mod is not None
    row = lambda w: pl.BlockSpec((tm, w), lambda i, dst: (i, 0))
    const = lambda *shape: pl.BlockSpec(shape, lambda i, dst: (0,) * len(shape))
    lift = lambda spec: pl.BlockSpec(spec.block_shape, (lambda f: (lambda i, dst: f(i)))(spec.index_map))
    in_specs = [pl.BlockSpec(memory_space=pl.ANY), row(d), row(LANES), lift(mod.spec(5, tm)), const(1, d), const(1, d)]
    args = [dest, y, x, top_g, mod.arg(5), ln_g.reshape(1, d), ln_b.reshape(1, d)]
    out_shape = [_sds((m, d))]
    out_specs = [row(d)]
    if has_next:
        in_specs += [lift(next_mod.spec(0, tm)), lift(next_mod.spec(1, tm))]
        args += [next_mod.arg(0), next_mod.arg(1)]
        out_shape.append(_sds((m, d), BF16))
        out_specs.append(row(d))
    out = pl.pallas_call(
        functools.partial(_combine_kernel, tok0=tok0, has_next=has_next),
        out_shape=tuple(out_shape),
        grid_spec=pltpu.PrefetchScalarGridSpec(
            num_scalar_prefetch=1, grid=(m // tm,),
            in_specs=in_specs, out_specs=tuple(out_specs),
            scratch_shapes=[pltpu.VMEM((TOP_K, tm, d), F32), pltpu.SemaphoreType.DMA(())]),
        compiler_params=_cparams(("arbitrary",), disable_bounds_checks=True),
        name="moe_combine_ln",
    )(*args)
    return (out[0], out[1]) if has_next else (out[0], None)


def moe_routing_tables(top_e):
    tm, tq = MOE_TM, MOE_Q
    n_tok = top_e.shape[0]
    nk = n_tok * TOP_K
    n_blocks = -(-nk // tm) + N_EXPERTS
    flat_e = top_e.reshape(-1)
    onehot = (flat_e[:, None] == jnp.arange(N_EXPERTS, dtype=I32)[None, :]).astype(I32)
    csum = jnp.cumsum(onehot, axis=0)
    counts = csum[-1]
    rank = jnp.sum(csum * onehot, axis=1) - 1
    blocks_per_e = (counts + tm - 1) // tm
    block_end = jnp.cumsum(blocks_per_e)
    block_start = block_end - blocks_per_e
    first_rows = counts - (blocks_per_e - 1) * tm
    fr = first_rows[flat_e]
    dest = (block_start[flat_e] * tm + jnp.where(rank < fr, rank, rank - fr + tm)).astype(I32)
    row_src = jnp.zeros((n_blocks * tm,), I32).at[dest].set(jnp.arange(nk, dtype=I32) // TOP_K)
    blk = jnp.arange(n_blocks, dtype=I32)
    n_live = block_end[-1].astype(I32)
    block_e = jnp.minimum(jnp.sum((blk[:, None] >= block_end[None, :]).astype(I32), axis=1), N_EXPERTS - 1)
    rows_here = jnp.where(blk == block_start[block_e], first_rows[block_e], tm)
    block_nsub = jnp.where(blk < n_live, jnp.clip((rows_here + tq - 1) // tq, 0, tm // tq), 0).astype(I32)
    block_e = jnp.where(blk < n_live, block_e, block_e[n_live - 1]).astype(I32)
    return dest, row_src, block_e, block_nsub, n_live.reshape(1)


def moe_sublayer(xs, h_all, tes, tgs, mods, next_mods, ln_g, ln_b, w_gu, b_gu, w_down, b_down, layer):
    top_e = jnp.concatenate([t[:, :TOP_K] for t in tes], axis=0)
    dest, row_src, block_e, block_nsub, n_live = moe_routing_tables(top_e)
    xsorted = moe_dispatch(row_src, block_nsub, h_all)
    y = moe_experts(xsorted, block_e, block_nsub, n_live, w_gu, b_gu, w_down, b_down, layer)
    outs, nexts = [], []
    tok0 = 0
    for grp in range(2):
        xo, hn = moe_combine_ln(dest, y, xs[grp], tgs[grp], mods[grp], ln_g, ln_b,
                                None if next_mods is None else next_mods[grp], tok0)
        tok0 += xs[grp].shape[0]
        outs.append(xo)
        nexts.append(hn)
    return outs, nexts


def _even_w_in(w):
    o = [0]
    for s in (2048, 512, 512, 1024, 64, 16, 2048, 3072, 32):
        o.append(o[-1] + s)
    q, k, v, qi, ki, wi, z, xbc, dt = [w[:, o[i]:o[i + 1]] for i in range(9)]
    pad = lambda t, n: jnp.concatenate([t, jnp.zeros((t.shape[0], n - t.shape[1]), t.dtype)], axis=1)
    cols = [q, z, xbc[:, :SSM_D], qi, xbc[:, SSM_D:], k, v, ki, ki, pad(wi, LANES), pad(dt, LANES)]
    return jnp.concatenate(cols, axis=1).astype(BF16)


def kernel(x_prompt, x_sample, cache_k, cache_v, cache_idx_k, state_ssm, state_ssm_conv, state_sconv, page_table,
           c_prompt, c_sample, ada_w, ada_b, ln_g, ln_b, even_w_in, even_w_out, ssm_conv_w, ssm_conv_b,
           ssm_dt_bias, ssm_a_log, ssm_d, ssm_norm_w, odd_w_in, odd_w_out, sconv_w, gmlp_ln_g, gmlp_ln_b,
           gmlp_ws, gmlp_bs, router_w, router_b, moe_w_gu, moe_b_gu, moe_w_down, moe_b_down):
    bp, seq, d = x_prompt.shape
    bs, dseq, _ = x_sample.shape
    depth = ada_w.shape[0]
    n_mod = 16
    c_all = jnp.zeros((n_mod, d), F32).at[:bp].set(c_prompt).at[bp:bp + bs].set(c_sample)
    m_all = adaln_all(c_all, ada_w, ada_b)
    mods = [[Mod(m_all[l, :bp], seq, False), Mod(m_all[l, bp:bp + bs], dseq, True)] for l in range(depth)]
    groups = [(bp, seq), (bs, dseq)]
    xs = [x_prompt.reshape(bp * seq, d), x_sample.reshape(bs * dseq, d)]
    hs = [modulate(xs[g], mods[0][g]) for g in range(2)]
    page_table = page_table.astype(I32)

    kp, vp, ikp, ks_, vs_, iks = [], [], [], [], [], []
    ssm_o = [[], []]
    ssmconv_o = [[], []]
    sconv_o = [[], []]
    gmv_o = [[], []]
    for layer in range(depth):
        j = layer // 2
        ya, yb = [None, None], [None, None]
        if layer % 2 == 0:
            w_in = _even_w_in(even_w_in[j])
            mamba_w = (ssm_conv_w[j], ssm_conv_b[j], ssm_dt_bias[j], ssm_a_log[j], ssm_d[j], ssm_norm_w[j])
            for g, (nb, sq) in enumerate(groups):
                proj = matmul(hs[g], w_in, EV_TN)
                p3 = proj.reshape(nb, sq, EV_N)
                k_new = p3[:, :, EV_K:EV_K + A_KV]
                v_new = p3[:, :, EV_V:EV_V + A_KV]
                ki_new = p3[:, :, EV_KK:EV_KK + IDX_DIM]
                xbc_tail = jnp.concatenate([p3[:, sq - (SSM_CONV - 1):, EV_XS:EV_XS + SSM_D],
                                            p3[:, sq - (SSM_CONV - 1):, EV_BC:EV_BC + BC_DIM]], axis=-1)
                if g == 0:
                    ya[g] = dsa_prompt(proj, nb, sq)
                    conv_prev = jnp.zeros((nb, SSM_CONV - 1, SSM_D + BC_DIM), F32)
                    h0 = jnp.zeros((nb, SSM_HEADS, SSM_HEAD_DIM, SSM_STATE), F32)
                    kp.append(k_new.reshape(nb, sq, ATT_KV_HEADS, HEAD_DIM))
                    vp.append(v_new.reshape(nb, sq, ATT_KV_HEADS, HEAD_DIM))
                    ikp.append(ki_new)
                else:
                    qs_rows = p3[:, :, EV_QI:EV_QI + IDX_HEADS * IDX_DIM].reshape(nb, sq * IDX_HEADS, IDX_DIM)
                    wcol = p3[:, :, EV_WI:EV_WI + IDX_HEADS].reshape(nb, sq * IDX_HEADS, 1)
                    scores = dsa_sample_scores(page_table, qs_rows, wcol, cache_idx_k, j)
                    padr = lambda t: jnp.concatenate(
                        [t, jnp.zeros((nb, LANES - sq, t.shape[2]), t.dtype)], axis=1)
                    att = dsa_sample_attend(page_table, p3[:, :, EV_Q:EV_Q + D_MODEL], qs_rows, wcol, padr(ki_new),
                                            scores, padr(k_new), padr(v_new), cache_k, cache_v, j)
                    ya[g] = att.reshape(nb * sq, D_MODEL)
                    conv_prev = state_ssm_conv[j]
                    h0 = state_ssm[j]
                    ks_.append(k_new.reshape(nb, sq, ATT_KV_HEADS, HEAD_DIM))
                    vs_.append(v_new.reshape(nb, sq, ATT_KV_HEADS, HEAD_DIM))
                    iks.append(ki_new)
                yb[g], h_last = ssd_mixer(proj, nb, sq, conv_prev, h0, *mamba_w)
                ssm_o[g].append(h_last)
                ssmconv_o[g].append(xbc_tail)
            w_out = even_w_out[j]
        else:
            w_in = odd_w_in[j].astype(BF16)
            for g, (nb, sq) in enumerate(groups):
                proj = matmul(hs[g], w_in, OD_TN)
                prev = jnp.zeros((nb, SC_WIDTH - 1, D_MODEL), F32) if g == 0 else state_sconv[j]
                ya[g], yb[g], sc_state, gm_v = odd_mixer(proj, nb, sq, prev, sconv_w[j], gmlp_ln_g[j], gmlp_ln_b[j],
                                                         gmlp_ws[j], gmlp_bs[j])
                sconv_o[g].append(sc_state)
                gmv_o[g].append(gm_v)
            w_out = odd_w_out[j]
        op_args = (ln_g[layer, 0], ln_b[layer, 0], router_w[layer], router_b[layer])
        x1s, hs_s, tes, tgs = outproj_ln_router(ya[1], yb[1], w_out, xs[1], mods[layer][1], *op_args)
        tail = jnp.zeros((OUTPROJ_TM, d), F32).at[:hs_s.shape[0]].set(hs_s)
        x1p, h_all, tep, tgp = outproj_ln_router(ya[0], yb[0], w_out, xs[0], mods[layer][0], *op_args, tail=tail)
        next_mods = mods[layer + 1] if layer + 1 < depth else None
        xs, hs = moe_sublayer([x1p, x1s], h_all, [tep, tes], [tgp, tgs], mods[layer], next_mods,
                              ln_g[layer, 1], ln_b[layer, 1], moe_w_gu, moe_b_gu, moe_w_down, moe_b_down, layer)
    return (xs[0].reshape(bp, seq, d), xs[1].reshape(bs, dseq, d),
            jnp.stack(kp), jnp.stack(vp), jnp.stack(ikp),
            jnp.stack(ks_), jnp.stack(vs_), jnp.stack(iks),
            jnp.stack(ssm_o[0]), jnp.stack(ssm_o[1]),
            jnp.stack(ssmconv_o[0]), jnp.stack(ssmconv_o[1]),
            jnp.stack(sconv_o[0]), jnp.stack(sconv_o[1]),
            jnp.stack(gmv_o[0]), jnp.stack(gmv_o[1]))
```

```python
import functools
import math

import jax
import jax.numpy as jnp
from jax import lax
from jax.experimental import pallas as pl
from jax.experimental.pallas import tpu as pltpu

F32 = jnp.float32
BF16 = jnp.bfloat16
I32 = jnp.int32

D_MODEL = 2048
DEPTH = 4
PAGE_SIZE = 128
HEAD_DIM = 128
ATT_HEADS = D_MODEL // HEAD_DIM
ATT_KV_HEADS = ATT_HEADS // 4
KV_REP = ATT_HEADS // ATT_KV_HEADS
IDX_HEADS = 16
IDX_DIM = 64
TOPK_MAX = 256
SSM_HEAD_DIM = 64
SSM_D = D_MODEL
SSM_HEADS = SSM_D // SSM_HEAD_DIM
SSM_GROUPS = 4
SSM_STATE = 128
SSM_CONV = 4
SSM_CHUNK = 128
SC_WIDTH = 3
GM_GROUPS = 16
GM_CHUNK = 128
N_EXPERTS = 32
TOP_K = 4
SWIGLU_LIMIT = 7.0
SWIGLU_ALPHA = 1.702
DN_ALPHA = (2 * DEPTH) ** 0.25
LN_EPS = 1e-5
RMS_EPS = 1e-5
A_KV = ATT_KV_HEADS * HEAD_DIM
BC_DIM = 2 * SSM_GROUPS * SSM_STATE
GROUP_COLS = SSM_D // SSM_GROUPS

LANES = 128
SUBLANES = 8
VMEM_LIMIT = 56 * 1024 * 1024

EV_Q, EV_Z, EV_XS, EV_QI, EV_BC, EV_K, EV_V, EV_KK, EV_WI, EV_DT, EV_N = (
    0, 2048, 4096, 6144, 7168, 8192, 8704, 9216, 9344, 9472, 9600)
EV_TN = 1920
OD_TN = 2048
DSA_SEG_BLOCKS = 2
OUTPROJ_TM = 256
MOE_TM = 1024
MOE_Q = 256
MOE_TN = 512
MOE_TN_DOWN = 1024
NEG_BIG = -1e30
INT_MIN = -2147483648


def _cparams(sem, **kw):
    return pltpu.CompilerParams(dimension_semantics=sem, vmem_limit_bytes=VMEM_LIMIT, **kw)


def _sds(shape, dtype=F32):
    return jax.ShapeDtypeStruct(shape, dtype)


def _silu(x):
    return x * jax.nn.sigmoid(x)


def _nt_dot(a, b):
    return lax.dot_general(a, b, (((1,), (1,)), ((), ())), preferred_element_type=F32)


def _dot(a, b):
    return jnp.dot(a, b, preferred_element_type=F32)


def _split3(x):
    hi = x.astype(BF16)
    r1 = x - hi.astype(F32)
    mid = r1.astype(BF16)
    lo = (r1 - mid.astype(F32)).astype(BF16)
    return hi, mid, lo


def _dot_exact_rhs(x, m_bf16):
    hi, mid, lo = _split3(x)
    return _dot(hi, m_bf16) + _dot(mid, m_bf16) + _dot(lo, m_bf16)


def _dot_exact_lhs(m_bf16, x):
    hi, mid, lo = _split3(x)
    return _dot(m_bf16, hi) + _dot(m_bf16, mid) + _dot(m_bf16, lo)


def _sort_key(x):
    b = pltpu.bitcast(x + 0.0, I32)
    return b ^ ((b >> 31) & jnp.int32(0x7FFFFFFF))


def _kth_largest_key(keys, k):
    rows = keys[0].shape[0]

    def body(it, p):
        cand = p | (jnp.int32(1) << (jnp.int32(31) - it))
        t = cand ^ jnp.int32(INT_MIN)
        cnt = jnp.zeros((rows, 1), I32)
        for kk in keys:
            cnt = cnt + jnp.sum((kk >= t).astype(I32), axis=-1, keepdims=True)
        return jnp.where(cnt >= k, cand, p)

    p = lax.fori_loop(0, 32, body, jnp.zeros((rows, 1), I32))
    return p ^ jnp.int32(INT_MIN)


def _adaln_kernel(c_ref, w_ref, b_ref, o_ref):
    s = _silu(c_ref[...]).astype(BF16)
    o_ref[...] = _dot(s, w_ref[...].astype(BF16)) + b_ref[...]


def adaln_all(c_all, ada_w, ada_b):
    depth, d, n = ada_w.shape
    rows = c_all.shape[0]
    tn = 1536
    return pl.pallas_call(
        _adaln_kernel,
        out_shape=_sds((depth, rows, n)),
        grid=(depth, n // tn),
        in_specs=[pl.BlockSpec((rows, d), lambda l, j: (0, 0)),
                  pl.BlockSpec((None, d, tn), lambda l, j: (l, 0, j)),
                  pl.BlockSpec((None, 1, tn), lambda l, j: (l, 0, j))],
        out_specs=pl.BlockSpec((None, rows, tn), lambda l, j: (l, 0, j)),
        compiler_params=_cparams(("arbitrary", "arbitrary")),
        name="adaln",
    )(c_all, ada_w, ada_b.reshape(depth, 1, n))


class Mod:
    def __init__(self, per_batch, rows_per_batch, expand):
        nb = per_batch.shape[0]
        self.rows_per_batch = rows_per_batch
        self.expand = expand
        if expand:
            m = per_batch.reshape(nb, 6, D_MODEL)
            self.rows = [jnp.repeat(m[:, w], rows_per_batch, axis=0) for w in range(6)]
        else:
            self.table = per_batch.reshape(nb * 6, 1, D_MODEL)

    def arg(self, which):
        return self.rows[which] if self.expand else self.table

    def spec(self, which, tm):
        if self.expand:
            return pl.BlockSpec((tm, D_MODEL), lambda i: (i, 0))
        rpb = self.rows_per_batch
        return pl.BlockSpec((None, 1, D_MODEL), lambda i: ((i * tm) // rpb * 6 + which, 0, 0))


def _row_tile(m, target):
    return min(m, target)


def _modulate_kernel(x_ref, sh_ref, sc_ref, o_ref):
    o_ref[...] = (x_ref[...] * (1.0 + sc_ref[...]) + sh_ref[...]).astype(o_ref.dtype)


def modulate(x, mod, out_dtype=BF16):
    m, d = x.shape
    tm = _row_tile(m, 512)
    return pl.pallas_call(
        _modulate_kernel,
        out_shape=_sds((m, d), out_dtype),
        grid=(m // tm,),
        in_specs=[pl.BlockSpec((tm, d), lambda i: (i, 0)), mod.spec(0, tm), mod.spec(1, tm)],
        out_specs=pl.BlockSpec((tm, d), lambda i: (i, 0)),
        compiler_params=_cparams(("arbitrary",)),
        name="modulate",
    )(x, mod.arg(0), mod.arg(1))


def _matmul_kernel(x_ref, w_ref, o_ref):
    o_ref[...] = _dot(x_ref[...], w_ref[...]).astype(o_ref.dtype)


def matmul(x, w, tn, out_dtype=F32):
    m, k = x.shape
    n = w.shape[1]
    tm = _row_tile(m, 512)
    return pl.pallas_call(
        _matmul_kernel,
        out_shape=_sds((m, n), out_dtype),
        grid=(n // tn, m // tm),
        in_specs=[pl.BlockSpec((tm, k), lambda j, i: (i, 0)),
                  pl.BlockSpec((k, tn), lambda j, i: (0, j))],
        out_specs=pl.BlockSpec((tm, tn), lambda j, i: (i, j)),
        compiler_params=_cparams(("arbitrary", "arbitrary")),
        name="in_proj",
    )(x, w)


def _layer_norm_rows(z, g, b):
    mu = jnp.mean(z, axis=-1, keepdims=True)
    zc = z - mu
    var = jnp.mean(zc * zc, axis=-1, keepdims=True)
    return zc * lax.rsqrt(var + LN_EPS) * g + b


def _top4_softmax(logits):
    tm = logits.shape[0]
    lane = lax.broadcasted_iota(I32, (tm, LANES), 1)
    v = logits
    vals, idxs = [], []
    for _ in range(TOP_K):
        m = jnp.max(v, axis=-1, keepdims=True)
        idx = jnp.min(jnp.where(v == m, lane, LANES), axis=-1, keepdims=True)
        vals.append(m)
        idxs.append(idx)
        v = jnp.where(lane == idx, -jnp.inf, v)
    es = [jnp.exp(m - vals[0]) for m in vals]
    den = es[0] + es[1] + es[2] + es[3]
    e_out = jnp.zeros((tm, LANES), I32)
    g_out = jnp.zeros((tm, LANES), F32)
    for k in range(TOP_K):
        e_out = jnp.where(lane == k, idxs[k], e_out)
        g_out = jnp.where(lane == k, es[k] / den, g_out)
    return e_out, g_out


def _outproj_kernel(ya_ref, yb_ref, wa_ref, wb_ref, x_ref, gate_ref, lng_ref, lnb_ref, sh_ref, sc_ref,
                    rw_ref, rb_ref, *rest, n_tiles, has_tail):
    if has_tail:
        tail_ref, xo_ref, h_ref, te_ref, tg_ref = rest
    else:
        xo_ref, h_ref, te_ref, tg_ref = rest

    def body():
        f = _dot(ya_ref[...], wa_ref[...]) + _dot(yb_ref[...], wb_ref[...])
        z = DN_ALPHA * x_ref[...] + (1.0 + gate_ref[...]) * f
        xn = _layer_norm_rows(z, lng_ref[...], lnb_ref[...])
        xo_ref[...] = xn
        h = xn * (1.0 + sc_ref[...]) + sh_ref[...]
        h_ref[...] = h
        logits = _dot(h.astype(BF16), rw_ref[...]) + rb_ref[...]
        e_out, g_out = _top4_softmax(logits)
        te_ref[...] = e_out
        tg_ref[...] = g_out

    if has_tail:
        i = pl.program_id(0)
        pl.when(i < n_tiles)(body)

        @pl.when(i == n_tiles)
        def _():
            h_ref[...] = tail_ref[...]
    else:
        body()


def outproj_ln_router(ya, yb, w_out, x, mod, ln_g, ln_b, router_w, router_b, tail=None):
    m, d = x.shape
    ka = ya.shape[1]
    tm = _row_tile(m, OUTPROJ_TM)
    n_tiles = m // tm
    has_tail = tail is not None
    wa = w_out[:ka].astype(BF16)
    wb = w_out[ka:].astype(BF16)
    rw = jnp.zeros((d, LANES), F32).at[:, :N_EXPERTS].set(router_w)
    rw = rw.astype(BF16)
    rb = jnp.full((1, LANES), NEG_BIG, F32).at[0, :N_EXPERTS].set(router_b)
    last = n_tiles - 1
    clamp = lambda spec: pl.BlockSpec(spec.block_shape, (lambda f: (lambda i: f(jnp.minimum(i, last))))(spec.index_map))
    const = lambda shape: pl.BlockSpec(shape, lambda i: (0,) * len(shape), pipeline_mode=pl.Buffered(1))
    row = lambda w: clamp(pl.BlockSpec((tm, w), lambda i: (i, 0)))
    in_specs = [row(ka), row(yb.shape[1]), const(wa.shape), const(wb.shape), row(d),
                clamp(mod.spec(2, tm)), const((1, d)), const((1, d)), clamp(mod.spec(3, tm)), clamp(mod.spec(4, tm)),
                const((d, LANES)), const((1, LANES))]
    args = [ya, yb, wa, wb, x, mod.arg(2), ln_g.reshape(1, d), ln_b.reshape(1, d), mod.arg(3), mod.arg(4),
            rw, rb]
    if has_tail:
        in_specs.append(const((tm, d)))
        args.append(tail)
    h_rows = m + (tm if has_tail else 0)
    return pl.pallas_call(
        functools.partial(_outproj_kernel, n_tiles=n_tiles, has_tail=has_tail),
        out_shape=(_sds((m, d)), _sds((h_rows, d)), _sds((m, LANES), I32), _sds((m, LANES))),
        grid=(n_tiles + (1 if has_tail else 0),),
        in_specs=in_specs,
        out_specs=(row(d), pl.BlockSpec((tm, d), lambda i: (i, 0)), row(LANES), row(LANES)),
        compiler_params=_cparams(("arbitrary",)),
        name="out_proj_ln_router",
    )(*args)


def _index_scores_block(qi_ref, wi_ref, kk):
    tq = qi_ref.shape[0]
    lane = lax.broadcasted_iota(I32, (tq, LANES), 1)
    wi = (wi_ref[...] * (IDX_HEADS ** -0.5)).astype(BF16).astype(F32)
    scores = None
    for p in range(IDX_HEADS // 2):
        qp = qi_ref[:, p * LANES:(p + 1) * LANES]
        for half in range(2):
            sel = (lane < IDX_DIM) if half == 0 else (lane >= IDX_DIM)
            qh = jnp.where(sel, qp, 0.0).astype(BF16)
            d = _nt_dot(qh, kk) * (IDX_DIM ** -0.5)
            h = 2 * p + half
            term = wi[:, h:h + 1] * jnp.maximum(d, 0.0).astype(BF16).astype(F32)
            scores = term if scores is None else scores + term
    return scores


def _dsa_prompt_kernel(q_ref, qi_ref, wi_ref, kk_ref, k_ref, v_ref, o_ref, *, n_sel, q0):
    tq = q_ref.shape[0]
    L = k_ref.shape[0]
    qb = pl.program_id(1)
    pos = q0 + qb * tq + lax.broadcasted_iota(I32, (tq, L), 0)
    kpos = lax.broadcasted_iota(I32, (tq, L), 1)
    valid = kpos <= pos
    if n_sel >= L:
        sel = valid
    else:
        scores = _index_scores_block(qi_ref, wi_ref, kk_ref[...].astype(BF16))
        key = jnp.where(valid, _sort_key(scores), _sort_key(jnp.full((1, 1), -jnp.inf, F32)))
        thr = _kth_largest_key([key], n_sel)
        sel = jnp.logical_and(key >= thr, valid)
    for g in range(ATT_KV_HEADS):
        kg = k_ref[:, g * HEAD_DIM:(g + 1) * HEAD_DIM].astype(BF16)
        vg = v_ref[:, g * HEAD_DIM:(g + 1) * HEAD_DIM].astype(BF16)
        for r in range(KV_REP):
            c0 = (g * KV_REP + r) * HEAD_DIM
            qh = q_ref[:, c0:c0 + HEAD_DIM].astype(BF16)
            logits = jnp.where(sel, _nt_dot(qh, kg) * (HEAD_DIM ** -0.5), NEG_BIG)
            m = jnp.max(logits, axis=-1, keepdims=True)
            p = jnp.exp(logits - m)
            l = jnp.sum(p, axis=-1, keepdims=True)
            o = _dot((p * (1.0 / l)).astype(BF16), vg)
            o_ref[:, c0:c0 + HEAD_DIM] = o.astype(o_ref.dtype)


def dsa_prompt(proj, n_batch, seq):
    tq = 128
    n_sel = min(TOPK_MAX, seq // 4)
    seg_blocks = min(DSA_SEG_BLOCKS, seq // tq)
    p3 = proj.reshape(n_batch, seq, proj.shape[1])
    outs = []
    for s in range(seq // (seg_blocks * tq)):
        q0 = s * seg_blocks * tq
        lc = q0 + seg_blocks * tq
        qrow = lambda w, col: pl.BlockSpec((None, tq, w), lambda b, i, s=s, col=col: (b, s * seg_blocks + i, col))
        keys = lambda w, col: pl.BlockSpec((None, lc, w), lambda b, i, col=col: (b, 0, col))
        outs.append(pl.pallas_call(
            functools.partial(_dsa_prompt_kernel, n_sel=n_sel, q0=q0),
            out_shape=_sds((n_batch, seg_blocks * tq, D_MODEL), BF16),
            grid=(n_batch, seg_blocks),
            in_specs=[qrow(D_MODEL, EV_Q // D_MODEL), qrow(1024, EV_QI // 1024), qrow(LANES, EV_WI // LANES),
                      keys(LANES, EV_KK // LANES), keys(A_KV, EV_K // A_KV), keys(A_KV, EV_V // A_KV)],
            out_specs=pl.BlockSpec((None, tq, D_MODEL), lambda b, i: (b, i, 0)),
            compiler_params=_cparams(("arbitrary", "arbitrary")),
            name="dsa_prompt",
        )(p3, p3, p3, p3, p3, p3))
    return jnp.concatenate(outs, axis=1).reshape(n_batch * seq, D_MODEL)


S1_PAGES = 16
S3_PAGES = 8


def _rows_index_scores(qs, wcol, keys_bf16):
    d = _nt_dot(qs, keys_bf16) * (IDX_DIM ** -0.5)
    w = wcol.astype(BF16).astype(F32) * jnp.maximum(d, 0.0).astype(BF16).astype(F32)
    nq = qs.shape[0] // IDX_HEADS
    return jnp.sum(w.reshape(nq, IDX_HEADS, w.shape[-1]), axis=1)


def _dsa_s1_kernel(pt_ref, qs_ref, wcol_ref, *refs):
    ik_refs, o_ref = refs[:S1_PAGES], refs[S1_PAGES]
    qs = qs_ref[...].astype(BF16)
    wcol = wcol_ref[...] * (IDX_HEADS ** -0.5)
    for i in range(S1_PAGES):
        o_ref[:, i * PAGE_SIZE:(i + 1) * PAGE_SIZE] = _rows_index_scores(qs, wcol, ik_refs[i][...].astype(BF16))


def dsa_sample_scores(page_table, qs_rows, wcol, pool_ik, layer_j):
    nb, n_pages = page_table.shape
    nq = qs_rows.shape[1] // IDX_HEADS
    steps = n_pages // S1_PAGES

    def ik_spec(i):
        return pl.BlockSpec((None, None, PAGE_SIZE, IDX_DIM),
                            lambda b, c, pt: (layer_j, pt[b, c * S1_PAGES + i], 0, 0))

    return pl.pallas_call(
        _dsa_s1_kernel,
        out_shape=_sds((nb, nq, n_pages * PAGE_SIZE)),
        grid_spec=pltpu.PrefetchScalarGridSpec(
            num_scalar_prefetch=1, grid=(nb, steps),
            in_specs=[pl.BlockSpec((None, nq * IDX_HEADS, IDX_DIM), lambda b, c, pt: (b, 0, 0)),
                      pl.BlockSpec((None, nq * IDX_HEADS, 1), lambda b, c, pt: (b, 0, 0))]
            + [ik_spec(i) for i in range(S1_PAGES)],
            out_specs=pl.BlockSpec((None, nq, S1_PAGES * PAGE_SIZE), lambda b, c, pt: (b, 0, c))),
        compiler_params=_cparams(("arbitrary", "arbitrary")),
        name="dsa_sample_scores",
    )(page_table, qs_rows, wcol, *([pool_ik] * S1_PAGES))


def _dsa_s3_kernel(pt_ref, q_ref, qs_ref, wcol_ref, kinew_ref, sall_ref, sblk_ref, knew_ref, vnew_ref, *refs,
                   n_sel, n_steps):
    kp = refs[:S3_PAGES]
    vp = refs[S3_PAGES:2 * S3_PAGES]
    o_ref = refs[2 * S3_PAGES]
    thr_s, newkey_s, m_s, l_s, acc_s = refs[2 * S3_PAGES + 1:]
    c = pl.program_id(1)
    nq = q_ref.shape[0]
    rows = KV_REP * nq
    neg_inf_key = _sort_key(jnp.full((1, 1), -jnp.inf, F32))

    def new_valid():
        t = lax.broadcasted_iota(I32, (nq, LANES), 0)
        s = lax.broadcasted_iota(I32, (nq, LANES), 1)
        return jnp.logical_and(s <= t, s < nq)

    @pl.when(c == 0)
    def _():
        wcol = wcol_ref[...] * (IDX_HEADS ** -0.5)
        s_new = _rows_index_scores(qs_ref[...].astype(BF16), wcol, kinew_ref[...].astype(BF16))
        nk = jnp.where(new_valid(), _sort_key(s_new), neg_inf_key)
        newkey_s[...] = nk
        thr_s[...] = _kth_largest_key([_sort_key(sall_ref[...]), nk], n_sel)
        m_s[...] = jnp.full(m_s.shape, NEG_BIG, F32)
        l_s[...] = jnp.zeros(l_s.shape, F32)
        acc_s[...] = jnp.zeros(acc_s.shape, F32)

    thr = thr_s[...]
    qg = [jnp.concatenate([q_ref[:, (g * KV_REP + r) * HEAD_DIM:(g * KV_REP + r + 1) * HEAD_DIM]
                           for r in range(KV_REP)], axis=0).astype(BF16) for g in range(ATT_KV_HEADS)]

    def update(g, kg, vg, sel_rows):
        logits = _nt_dot(qg[g], kg) * (HEAD_DIM ** -0.5)
        lm = jnp.where(sel_rows, logits, NEG_BIG)
        m_old = m_s[g]
        m_new = jnp.maximum(m_old, jnp.max(lm, axis=-1, keepdims=True))
        p = jnp.where(sel_rows, jnp.exp(lm - m_new), 0.0)
        a = jnp.exp(m_old - m_new)
        l_s[g] = a * l_s[g] + jnp.sum(p, axis=-1, keepdims=True)
        acc_s[g] = a * acc_s[g] + _dot(p.astype(BF16), vg)
        m_s[g] = m_new

    def tile_rows(sel):
        f = jnp.where(sel, 1.0, 0.0)
        return jnp.concatenate([f] * KV_REP, axis=0) > 0.5

    def head_rows(page_refs, g):
        return jnp.concatenate([r[pl.ds(g, PAGE_SIZE, stride=ATT_KV_HEADS), :] for r in page_refs],
                               axis=0).astype(BF16)

    sel = tile_rows(_sort_key(sblk_ref[...]) >= thr)
    for g in range(ATT_KV_HEADS):
        update(g, head_rows(kp, g), head_rows(vp, g), sel)

    @pl.when(c == n_steps - 1)
    def _():
        sel = tile_rows(jnp.logical_and(newkey_s[...] >= thr, new_valid()))
        for g in range(ATT_KV_HEADS):
            update(g, knew_ref[:, g * HEAD_DIM:(g + 1) * HEAD_DIM].astype(BF16),
                   vnew_ref[:, g * HEAD_DIM:(g + 1) * HEAD_DIM].astype(BF16), sel)
        for g in range(ATT_KV_HEADS):
            o = acc_s[g] / l_s[g]
            for r in range(KV_REP):
                c0 = (g * KV_REP + r) * HEAD_DIM
                o_ref[:, c0:c0 + HEAD_DIM] = o[r * nq:(r + 1) * nq].astype(o_ref.dtype)


def dsa_sample_attend(page_table, q, qs_rows, wcol, ki_new, scores, k_new, v_new, pool_k, pool_v, layer_j):
    nb, n_pages = page_table.shape
    nq = q.shape[1]
    past = n_pages * PAGE_SIZE
    n_sel = min(TOPK_MAX, (past + nq) // 4)
    n_steps = n_pages // S3_PAGES
    rows = KV_REP * nq

    def page_spec(i):
        return pl.BlockSpec((None, None, PAGE_SIZE * ATT_KV_HEADS, HEAD_DIM),
                            lambda b, c, pt: (layer_j, pt[b, c * S3_PAGES + i], 0, 0))

    pool_k = pool_k.reshape(pool_k.shape[0], pool_k.shape[1], PAGE_SIZE * ATT_KV_HEADS, HEAD_DIM)
    pool_v = pool_v.reshape(pool_v.shape[0], pool_v.shape[1], PAGE_SIZE * ATT_KV_HEADS, HEAD_DIM)
    per_b = lambda *shape: pl.BlockSpec((None,) + shape, lambda b, c, pt: (b,) + (0,) * len(shape))
    return pl.pallas_call(
        functools.partial(_dsa_s3_kernel, n_sel=n_sel, n_steps=n_steps),
        out_shape=_sds((nb, nq, D_MODEL), BF16),
        grid_spec=pltpu.PrefetchScalarGridSpec(
            num_scalar_prefetch=1, grid=(nb, n_steps),
            in_specs=[per_b(nq, D_MODEL), per_b(nq * IDX_HEADS, IDX_DIM), per_b(nq * IDX_HEADS, 1),
                      per_b(LANES, IDX_DIM), per_b(nq, past),
                      pl.BlockSpec((None, nq, S3_PAGES * PAGE_SIZE), lambda b, c, pt: (b, 0, c)),
                      per_b(LANES, A_KV), per_b(LANES, A_KV)]
            + [page_spec(i) for i in range(S3_PAGES)] * 2,
            out_specs=per_b(nq, D_MODEL),
            scratch_shapes=[pltpu.VMEM((nq, 1), I32), pltpu.VMEM((nq, LANES), I32),
                            pltpu.VMEM((ATT_KV_HEADS, rows, 1), F32), pltpu.VMEM((ATT_KV_HEADS, rows, 1), F32),
                            pltpu.VMEM((ATT_KV_HEADS, rows, HEAD_DIM), F32)]),
        compiler_params=_cparams(("arbitrary", "arbitrary")),
        name="dsa_sample_attend",
    )(page_table, q, qs_rows, wcol, ki_new, scores, scores, k_new, v_new,
      *([pool_k] * S3_PAGES), *([pool_v] * S3_PAGES))


def _conv_rows(buf, w_ref, rows, width):
    base = SUBLANES - (width - 1)
    out = None
    for j in range(width):
        term = w_ref[j:j + 1, :] * buf[base + j:base + j + rows, :]
        out = term if out is None else out + term
    return out


def _ssd_kernel(z_ref, xs_ref, bc_ref, dt_ref, prevx_ref, prevb_ref, h0_ref,
                cwx_ref, cwb_ref, cbx_ref, cbb_ref, dtb_ref, alog_ref, dskip_ref, nw_ref, e_ref,
                y_ref, hl_ref, bufx, bufb, xs_s, bc_s, st_s, *, rows, n_chunks):
    c = pl.program_id(1)
    ck = SSM_CHUNK

    @pl.when(c == 0)
    def _():
        bufx[0:SUBLANES] = prevx_ref[...]
        bufb[0:SUBLANES] = prevb_ref[...]
        st_s[...] = h0_ref[...]

    @pl.when(c > 0)
    def _():
        bufx[0:SUBLANES] = bufx[rows:rows + SUBLANES]
        bufb[0:SUBLANES] = bufb[rows:rows + SUBLANES]

    bufx[SUBLANES:SUBLANES + rows] = xs_ref[...]
    bufb[SUBLANES:SUBLANES + rows] = bc_ref[...]
    xc = _silu(_conv_rows(bufx, cwx_ref, rows, SSM_CONV) + cbx_ref[...])
    bcc = _silu(_conv_rows(bufb, cwb_ref, rows, SSM_CONV) + cbb_ref[...])
    if rows < ck:
        xs_s[...] = jnp.zeros(xs_s.shape, F32)
        bc_s[...] = jnp.zeros(bc_s.shape, F32)
    xs_s[0:rows] = xc
    bc_s[0:rows] = bcc

    dt_raw = dt_ref[...] + dtb_ref[...]
    dt = jnp.maximum(dt_raw, 0.0) + jnp.log1p(jnp.exp(-jnp.abs(dt_raw)))
    if rows < ck:
        dt = jnp.concatenate([dt, jnp.zeros((ck - rows, LANES), F32)], axis=0)
    a = -jnp.exp(alog_ref[...])
    ri = lax.broadcasted_iota(I32, (ck, ck), 0)
    ci = lax.broadcasted_iota(I32, (ck, ck), 1)
    tril = ri >= ci
    tril_b = jnp.where(tril, 1.0, 0.0).astype(BF16)
    acum = _dot_exact_lhs(tril_b, dt * a)
    acum_t = acum.T
    e = e_ref[...]
    dtx = _dot_exact_rhs(dt, e)
    acx = _dot_exact_rhs(acum, e)
    eax = jnp.exp(acx)
    last = acx[ck - 1:ck, :]
    dendx = jnp.exp(last - acx)
    cdecx = jnp.exp(last)

    xs = xs_s[...]
    xdt = xs * dtx
    xdt_b = xdt.astype(BF16)
    xdd_b = (xdt * dendx).astype(BF16)
    lane = lax.broadcasted_iota(I32, (ck, LANES), 1)
    lo_half = lane < SSM_HEAD_DIM
    hpg = SSM_HEADS // SSM_GROUPS
    y_parts = []
    for g in range(SSM_GROUPS):
        bm = bc_s[:, g * SSM_STATE:(g + 1) * SSM_STATE]
        cm = bc_s[:, (SSM_GROUPS + g) * SSM_STATE:(SSM_GROUPS + g + 1) * SSM_STATE]
        bm_b = bm.astype(BF16)
        cm_b = cm.astype(BF16)
        cb = _nt_dot(cm_b, bm_b)
        g0 = g * GROUP_COLS
        st = st_s[g]
        y_off = _dot(cm_b, st.astype(BF16)) * eax[:, g0:g0 + GROUP_COLS]
        pair_out = []
        for pr in range(hpg // 2):
            res = []
            for half in range(2):
                h = g * hpg + 2 * pr + half
                seg = acum[:, h:h + 1] - acum_t[h:h + 1, :]
                decay = jnp.exp(jnp.where(tril, seg, -jnp.inf))
                sc = (cb * decay).astype(BF16)
                res.append(_dot(sc, xdt_b[:, g0 + pr * LANES:g0 + (pr + 1) * LANES]))
            pair_out.append(jnp.where(lo_half, res[0], res[1]))
        y_parts.append(jnp.concatenate(pair_out, axis=1) + y_off)
        new_st = _dot(bm.T.astype(BF16), xdd_b[:, g0:g0 + GROUP_COLS])
        st_s[g] = st * cdecx[:, g0:g0 + GROUP_COLS] + new_st

    zz = z_ref[...]
    nw = nw_ref[...]
    for g in range(SSM_GROUPS):
        g0 = g * GROUP_COLS
        y = y_parts[g][0:rows] + dskip_ref[:, g0:g0 + GROUP_COLS] * xs[0:rows, g0:g0 + GROUP_COLS]
        y = y * _silu(zz[:, g0:g0 + GROUP_COLS])
        y = y * lax.rsqrt(jnp.mean(y * y, axis=-1, keepdims=True) + RMS_EPS)
        y_ref[:, g0:g0 + GROUP_COLS] = (y * nw[:, g0:g0 + GROUP_COLS]).astype(y_ref.dtype)

    @pl.when(c == n_chunks - 1)
    def _():
        hl_ref[...] = st_s[...]


def _pad_prev(prev):
    b, w1, ch = prev.shape
    return jnp.concatenate([jnp.zeros((b, SUBLANES - w1, ch), prev.dtype), prev], axis=1)


def _pad_rows(w, rows=SUBLANES):
    return jnp.concatenate([w, jnp.zeros((rows - w.shape[0], w.shape[1]), w.dtype)], axis=0)


def _pad_lanes(v, n=LANES):
    return jnp.zeros((1, n), v.dtype).at[0, :v.shape[0]].set(v)


def ssd_mixer(proj, n_batch, seq, conv_prev, h0, conv_w, conv_b, dt_bias, a_log, d_skip, norm_w):
    rows = SSM_CHUNK if seq % SSM_CHUNK == 0 else seq
    n_chunks = seq // rows
    hpg = SSM_HEADS // SSM_GROUPS
    prev = _pad_prev(conv_prev)
    prevx, prevb = prev[:, :, :SSM_D], prev[:, :, SSM_D:]
    h0k = h0.reshape(n_batch, SSM_GROUPS, hpg, SSM_HEAD_DIM, SSM_STATE).transpose(0, 1, 4, 2, 3)
    h0k = h0k.reshape(n_batch, SSM_GROUPS, SSM_STATE, GROUP_COLS)
    cw = _pad_rows(conv_w)
    e = (jnp.arange(LANES)[:, None] == (jnp.arange(SSM_D)[None, :] // SSM_HEAD_DIM)).astype(BF16)
    nb = n_chunks
    row = lambda w, col: pl.BlockSpec((rows, w), lambda b, c: (b * nb + c, col))
    per_b = lambda *shape: pl.BlockSpec((None,) + shape, lambda b, c: (b,) + (0,) * len(shape))
    const = lambda *shape: pl.BlockSpec(shape, lambda b, c: (0,) * len(shape))
    y, hl = pl.pallas_call(
        functools.partial(_ssd_kernel, rows=rows, n_chunks=n_chunks),
        out_shape=(_sds((n_batch * seq, SSM_D), BF16),
                   _sds((n_batch, SSM_GROUPS, SSM_STATE, GROUP_COLS))),
        grid=(n_batch, n_chunks),
        in_specs=[row(SSM_D, EV_Z // SSM_D), row(SSM_D, EV_XS // SSM_D), row(BC_DIM, EV_BC // BC_DIM),
                  row(LANES, EV_DT // LANES),
                  per_b(SUBLANES, SSM_D), per_b(SUBLANES, BC_DIM), per_b(SSM_GROUPS, SSM_STATE, GROUP_COLS),
                  const(SUBLANES, SSM_D), const(SUBLANES, BC_DIM), const(1, SSM_D), const(1, BC_DIM),
                  const(1, LANES), const(1, LANES), const(1, SSM_D), const(1, SSM_D), const(LANES, SSM_D)],
        out_specs=(pl.BlockSpec((rows, SSM_D), lambda b, c: (b * nb + c, 0)),
                   per_b(SSM_GROUPS, SSM_STATE, GROUP_COLS)),
        scratch_shapes=[pltpu.VMEM((rows + SUBLANES, SSM_D), F32), pltpu.VMEM((rows + SUBLANES, BC_DIM), F32),
                        pltpu.VMEM((SSM_CHUNK, SSM_D), F32), pltpu.VMEM((SSM_CHUNK, BC_DIM), F32),
                        pltpu.VMEM((SSM_GROUPS, SSM_STATE, GROUP_COLS), F32)],
        compiler_params=_cparams(("arbitrary", "arbitrary")),
        name="ssd_mixer",
    )(proj, proj, proj, proj, prevx, prevb, h0k,
      cw[:, :SSM_D], cw[:, SSM_D:], conv_b[None, :SSM_D], conv_b[None, SSM_D:],
      _pad_lanes(dt_bias), _pad_lanes(a_log), jnp.repeat(d_skip, SSM_HEAD_DIM)[None, :], norm_w[None, :], e)
    h_last = hl.reshape(n_batch, SSM_GROUPS, SSM_STATE, hpg, SSM_HEAD_DIM).transpose(0, 1, 3, 4, 2)
    return y, h_last.reshape(n_batch, SSM_HEADS, SSM_HEAD_DIM, SSM_STATE)


def _odd_kernel(bg_ref, cg_ref, sv_ref, u_ref, v_ref, prev_ref, scw_ref, lng_ref, lnb_ref, ws_ref, bst_ref,
                sc_ref, gm_ref, scst_ref, gmv_ref, buf, vpad, *, rows, n_chunks):
    c = pl.program_id(1)
    ck = GM_CHUNK

    @pl.when(c == 0)
    def _():
        buf[0:SUBLANES] = prev_ref[...]

    @pl.when(c > 0)
    def _():
        buf[0:SUBLANES] = buf[rows:rows + SUBLANES]

    buf[SUBLANES:SUBLANES + rows] = cg_ref[...] * sv_ref[...]
    sc_ref[...] = (bg_ref[...] * _conv_rows(buf, scw_ref, rows, SC_WIDTH)).astype(sc_ref.dtype)

    def gelu(x):
        return 0.5 * x * (1.0 + lax.erf(x * (0.5 ** 0.5)))

    u = gelu(u_ref[...])
    v = _layer_norm_rows(gelu(v_ref[...]), lng_ref[...], lnb_ref[...])
    if rows < ck:
        vpad[...] = jnp.zeros(vpad.shape, F32)
    vpad[0:rows] = v
    ri = lax.broadcasted_iota(I32, (ck, ck), 0)
    ci = lax.broadcasted_iota(I32, (ck, ck), 1)
    tril = ri >= ci
    gd = D_MODEL // GM_GROUPS
    for g in range(GM_GROUPS):
        wm = jnp.where(tril, ws_ref[g], 0.0).astype(BF16)
        mixed = _dot(wm, vpad[:, g * gd:(g + 1) * gd].astype(BF16)) + bst_ref[:, g:g + 1]
        gm_ref[:, g * gd:(g + 1) * gd] = (u[:, g * gd:(g + 1) * gd] * mixed[0:rows]).astype(gm_ref.dtype)

    @pl.when(c == n_chunks - 1)
    def _():
        scst_ref[...] = buf[rows:rows + SUBLANES]
        gmv_ref[...] = v


def odd_mixer(proj, n_batch, seq, sconv_prev, sconv_w, gm_ln_g, gm_ln_b, gm_ws, gm_bs):
    d = D_MODEL
    rows = GM_CHUNK if seq % GM_CHUNK == 0 else seq
    n_chunks = seq // rows
    nb = n_chunks
    prev = _pad_prev(sconv_prev)
    bst = jnp.zeros((GM_CHUNK, LANES), F32).at[:, :GM_GROUPS].set(gm_bs.T)
    row = lambda col: pl.BlockSpec((rows, d), lambda b, c: (b * nb + c, col))
    per_b = lambda *shape: pl.BlockSpec((None,) + shape, lambda b, c: (b,) + (0,) * len(shape))
    const = lambda *shape: pl.BlockSpec(shape, lambda b, c: (0,) * len(shape))
    sc, gm, scst, gmv = pl.pallas_call(
        functools.partial(_odd_kernel, rows=rows, n_chunks=n_chunks),
        out_shape=(_sds((n_batch * seq, d), BF16), _sds((n_batch * seq, d), BF16),
                   _sds((n_batch, SUBLANES, d)), _sds((n_batch, rows, d))),
        grid=(n_batch, n_chunks),
        in_specs=[row(0), row(1), row(2), row(3), row(4), per_b(SUBLANES, d), const(SUBLANES, d),
                  const(1, d), const(1, d), const(GM_GROUPS, GM_CHUNK, GM_CHUNK), const(GM_CHUNK, LANES)],
        out_specs=(pl.BlockSpec((rows, d), lambda b, c: (b * nb + c, 0)),
                   pl.BlockSpec((rows, d), lambda b, c: (b * nb + c, 0)),
                   per_b(SUBLANES, d), per_b(rows, d)),
        scratch_shapes=[pltpu.VMEM((rows + SUBLANES, d), F32), pltpu.VMEM((GM_CHUNK, d), F32)],
        compiler_params=_cparams(("arbitrary", "arbitrary")),
        name="odd_mixer",
    )(proj, proj, proj, proj, proj, prev, _pad_rows(sconv_w), gm_ln_g[None, :], gm_ln_b[None, :], gm_ws, bst)
    return sc, gm, scst[:, SUBLANES - (SC_WIDTH - 1):], gmv


def _row_copy(src_hbm, row, dst, slot, sem):
    return pltpu.make_async_copy(src_hbm.at[pl.ds(row, 1), :], dst.at[pl.ds(slot, 1), :], sem)


DISPATCH_UNROLL = 8


def _dispatch_kernel(nsub_ref, src_ref, h_hbm, o_ref, buf, sem):
    i = pl.program_id(0)
    tq = MOE_Q
    ns = nsub_ref[i]

    n_sub_max = buf.shape[0] // tq
    for q in range(n_sub_max):
        @pl.when(q < ns)
        def _(q=q):
            def issue(t8, carry):
                for u in range(DISPATCH_UNROLL):
                    t = q * tq + t8 * DISPATCH_UNROLL + u
                    _row_copy(h_hbm, src_ref[0, t], buf, t, sem.at[q]).start()
                return carry

            lax.fori_loop(0, tq // DISPATCH_UNROLL, issue, 0)

    for q in range(n_sub_max):
        @pl.when(q < ns)
        def _(q=q):
            pltpu.make_async_copy(h_hbm.at[pl.ds(0, tq), :], buf.at[pl.ds(q * tq, tq), :], sem.at[q]).wait()
            o_ref[q * tq:(q + 1) * tq] = buf[q * tq:(q + 1) * tq].astype(o_ref.dtype)

        @pl.when(q >= ns)
        def _(q=q):
            o_ref[q * tq:(q + 1) * tq] = jnp.zeros((tq, o_ref.shape[1]), o_ref.dtype)


def moe_dispatch(row_src, block_nsub, h_all):
    d = h_all.shape[1]
    tm = MOE_TM
    n_blocks = block_nsub.shape[0]
    return pl.pallas_call(
        _dispatch_kernel,
        out_shape=_sds((n_blocks * tm, d), BF16),
        grid_spec=pltpu.PrefetchScalarGridSpec(
            num_scalar_prefetch=1, grid=(n_blocks,),
            in_specs=[pl.BlockSpec((None, 1, tm), lambda i, ns: (i, 0, 0), memory_space=pltpu.SMEM),
                      pl.BlockSpec(memory_space=pl.ANY)],
            out_specs=pl.BlockSpec((tm, d), lambda i, ns: (i, 0)),
            scratch_shapes=[pltpu.VMEM((tm, d), F32), pltpu.SemaphoreType.DMA((tm // MOE_Q,))]),
        compiler_params=_cparams(("arbitrary",), disable_bounds_checks=True),
        name="moe_dispatch",
    )(block_nsub, row_src.reshape(n_blocks, 1, tm), h_all)


def _first_of_expert(be_ref, i):
    prev = be_ref[jnp.maximum(i - 1, 0)]
    return jnp.logical_or(i == 0, be_ref[i] != prev)


def _for_live_rows(nsub_ref, i, o_ref, compute):
    n_sub_max = o_ref.shape[0] // MOE_Q
    ns = nsub_ref[i]
    for q in range(1, n_sub_max + 1):
        @pl.when(ns == q)
        def _(q=q):
            rows = q * MOE_Q
            o_ref[0:rows] = compute(rows).astype(o_ref.dtype)
            if q < n_sub_max:
                o_ref[rows:] = jnp.zeros((o_ref.shape[0] - rows, o_ref.shape[1]), o_ref.dtype)

    @pl.when(ns == 0)
    def _():
        o_ref[...] = jnp.zeros(o_ref.shape, o_ref.dtype)


def _expert_gu_kernel(be_ref, nsub_ref, nl_ref, x_ref, wg_ref, wu_ref, bg_ref, bu_ref, o_ref, wg_s, wu_s):
    i = pl.program_id(1)

    @pl.when(_first_of_expert(be_ref, i))
    def _():
        wg_s[...] = wg_ref[...].astype(BF16)
        wu_s[...] = wu_ref[...].astype(BF16)

    def compute(rows):
        x = x_ref[0:rows]
        g = _dot(x, wg_s[...]) + bg_ref[...]
        u = _dot(x, wu_s[...]) + bu_ref[...]
        g = jnp.minimum(g, SWIGLU_LIMIT)
        u = jnp.clip(u, -SWIGLU_LIMIT, SWIGLU_LIMIT)
        return (u + 1.0) * (g * jax.nn.sigmoid(SWIGLU_ALPHA * g))

    _for_live_rows(nsub_ref, i, o_ref, compute)


def _expert_down_kernel(be_ref, nsub_ref, nl_ref, a_ref, w_ref, b_ref, o_ref, w_s):
    i = pl.program_id(1)

    @pl.when(_first_of_expert(be_ref, i))
    def _():
        w_s[...] = w_ref[...].astype(BF16)

    _for_live_rows(nsub_ref, i, o_ref, lambda rows: _dot(a_ref[0:rows], w_s[...]) + b_ref[...])


def moe_experts(xs, block_e, block_nsub, n_live, w_gu, b_gu, w_down, b_down, layer):
    n_rows, d = xs.shape
    tm, tn = MOE_TM, MOE_TN
    n_blocks = n_rows // tm
    de = w_down.shape[2]
    nj = de // tn
    live = lambda i, nl: jnp.minimum(i, nl[0] - 1)
    act = pl.pallas_call(
        _expert_gu_kernel,
        out_shape=_sds((n_rows, de), BF16),
        grid_spec=pltpu.PrefetchScalarGridSpec(
            num_scalar_prefetch=3, grid=(nj, n_blocks),
            in_specs=[pl.BlockSpec((tm, d), lambda j, i, be, ns, nl: (live(i, nl), 0)),
                      pl.BlockSpec((None, None, d, tn), lambda j, i, be, ns, nl: (layer, be[i], 0, j)),
                      pl.BlockSpec((None, None, d, tn), lambda j, i, be, ns, nl: (layer, be[i], 0, nj + j)),
                      pl.BlockSpec((None, None, 1, tn), lambda j, i, be, ns, nl: (layer, be[i], 0, j)),
                      pl.BlockSpec((None, None, 1, tn), lambda j, i, be, ns, nl: (layer, be[i], 0, nj + j))],
            out_specs=pl.BlockSpec((tm, tn), lambda j, i, be, ns, nl: (i, j)),
            scratch_shapes=[pltpu.VMEM((d, tn), BF16), pltpu.VMEM((d, tn), BF16)]),
        compiler_params=_cparams(("arbitrary", "arbitrary")),
        name="moe_expert_gu",
    )(block_e, block_nsub, n_live, xs, w_gu, w_gu, b_gu.reshape(b_gu.shape[0], b_gu.shape[1], 1, -1),
      b_gu.reshape(b_gu.shape[0], b_gu.shape[1], 1, -1))
    dn = w_down.shape[3]
    tn = MOE_TN_DOWN
    njd = dn // tn
    return pl.pallas_call(
        _expert_down_kernel,
        out_shape=_sds((n_rows, dn)),
        grid_spec=pltpu.PrefetchScalarGridSpec(
            num_scalar_prefetch=3, grid=(njd, n_blocks),
            in_specs=[pl.BlockSpec((tm, de), lambda j, i, be, ns, nl: (live(i, nl), 0)),
                      pl.BlockSpec((None, None, de, tn), lambda j, i, be, ns, nl: (layer, be[i], 0, j)),
                      pl.BlockSpec((None, None, 1, tn), lambda j, i, be, ns, nl: (layer, be[i], 0, j))],
            out_specs=pl.BlockSpec((tm, tn), lambda j, i, be, ns, nl: (i, j)),
            scratch_shapes=[pltpu.VMEM((de, tn), BF16)]),
        compiler_params=_cparams(("arbitrary", "arbitrary")),
        name="moe_expert_down",
    )(block_e, block_nsub, n_live, act, w_down, b_down.reshape(b_down.shape[0], b_down.shape[1], 1, -1))


def _combine_kernel(dest_ref, y_hbm, x_ref, tg_ref, gate_ref, lng_ref, lnb_ref, *rest, tok0, has_next):
    if has_next:
        sh_ref, sc_ref, xo_ref, ho_ref, buf, sem = rest
    else:
        xo_ref, buf, sem = rest
    i = pl.program_id(0)
    tm = x_ref.shape[0]

    def issue(t2, carry):
        for u in range(2):
            t = t2 * 2 + u
            base = (tok0 + i * tm + t) * TOP_K
            for k in range(TOP_K):
                _row_copy(y_hbm, dest_ref[base + k], buf.at[k], t, sem).start()
        return carry

    lax.fori_loop(0, tm // 2, issue, 0)
    for k in range(TOP_K):
        pltpu.make_async_copy(y_hbm.at[pl.ds(0, tm), :], buf.at[k], sem).wait()
    tg = tg_ref[...]
    f = tg[:, 0:1] * buf[0]
    for k in range(1, TOP_K):
        f = f + tg[:, k:k + 1] * buf[k]
    z = DN_ALPHA * x_ref[...] + (1.0 + gate_ref[...]) * f
    xn = _layer_norm_rows(z, lng_ref[...], lnb_ref[...])
    xo_ref[...] = xn
    if has_next:
        ho_ref[...] = (xn * (1.0 + sc_ref[...]) + sh_ref[...]).astype(ho_ref.dtype)


def moe_combine_ln(dest, y, x, top_g, mod, ln_g, ln_b, next_mod, tok0):
    m, d = x.shape
    tm = _row_tile(m, 256)
    has_next = next_mod is not None
    row = lambda w: pl.BlockSpec((tm, w), lambda i, dst: (i, 0))
    const = lambda *shape: pl.BlockSpec(shape, lambda i, dst: (0,) * len(shape))
    lift = lambda spec: pl.BlockSpec(spec.block_shape, (lambda f: (lambda i, dst: f(i)))(spec.index_map))
    in_specs = [pl.BlockSpec(memory_space=pl.ANY), row(d), row(LANES), lift(mod.spec(5, tm)), const(1, d), const(1, d)]
    args = [dest, y, x, top_g, mod.arg(5), ln_g.reshape(1, d), ln_b.reshape(1, d)]
    out_shape = [_sds((m, d))]
    out_specs = [row(d)]
    if has_next:
        in_specs += [lift(next_mod.spec(0, tm)), lift(next_mod.spec(1, tm))]
        args += [next_mod.arg(0), next_mod.arg(1)]
        out_shape.append(_sds((m, d), BF16))
        out_specs.append(row(d))
    out = pl.pallas_call(
        functools.partial(_combine_kernel, tok0=tok0, has_next=has_next),
        out_shape=tuple(out_shape),
        grid_spec=pltpu.PrefetchScalarGridSpec(
            num_scalar_prefetch=1, grid=(m // tm,),
            in_specs=in_specs, out_specs=tuple(out_specs),
            scratch_shapes=[pltpu.VMEM((TOP_K, tm, d), F32), pltpu.SemaphoreType.DMA(())]),
        compiler_params=_cparams(("arbitrary",), disable_bounds_checks=True),
        name="moe_combine_ln",
    )(*args)
    return (out[0], out[1]) if has_next else (out[0], None)


def moe_routing_tables(top_e):
    tm, tq = MOE_TM, MOE_Q
    n_tok = top_e.shape[0]
    nk = n_tok * TOP_K
    n_blocks = -(-nk // tm) + N_EXPERTS
    flat_e = top_e.reshape(-1)
    onehot = (flat_e[:, None] == jnp.arange(N_EXPERTS, dtype=I32)[None, :]).astype(I32)
    csum = jnp.cumsum(onehot, axis=0)
    counts = csum[-1]
    rank = jnp.sum(csum * onehot, axis=1) - 1
    blocks_per_e = (counts + tm - 1) // tm
    block_end = jnp.cumsum(blocks_per_e)
    block_start = block_end - blocks_per_e
    first_rows = counts - (blocks_per_e - 1) * tm
    fr = first_rows[flat_e]
    dest = (block_start[flat_e] * tm + jnp.where(rank < fr, rank, rank - fr + tm)).astype(I32)
    row_src = jnp.zeros((n_blocks * tm,), I32).at[dest].set(jnp.arange(nk, dtype=I32) // TOP_K)
    blk = jnp.arange(n_blocks, dtype=I32)
    n_live = block_end[-1].astype(I32)
    block_e = jnp.minimum(jnp.sum((blk[:, None] >= block_end[None, :]).astype(I32), axis=1), N_EXPERTS - 1)
    rows_here = jnp.where(blk == block_start[block_e], first_rows[block_e], tm)
    block_nsub = jnp.where(blk < n_live, jnp.clip((rows_here + tq - 1) // tq, 0, tm // tq), 0).astype(I32)
    block_e = jnp.where(blk < n_live, block_e, block_e[n_live - 1]).astype(I32)
    return dest, row_src, block_e, block_nsub, n_live.reshape(1)


def moe_sublayer(xs, h_all, tes, tgs, mods, next_mods, ln_g, ln_b, w_gu, b_gu, w_down, b_down, layer):
    top_e = jnp.concatenate([t[:, :TOP_K] for t in tes], axis=0)
    dest, row_src, block_e, block_nsub, n_live = moe_routing_tables(top_e)
    xsorted = moe_dispatch(row_src, block_nsub, h_all)
    y = moe_experts(xsorted, block_e, block_nsub, n_live, w_gu, b_gu, w_down, b_down, layer)
    outs, nexts = [], []
    tok0 = 0
    for grp in range(2):
        xo, hn = moe_combine_ln(dest, y, xs[grp], tgs[grp], mods[grp], ln_g, ln_b,
                                None if next_mods is None else next_mods[grp], tok0)
        tok0 += xs[grp].shape[0]
        outs.append(xo)
        nexts.append(hn)
    return outs, nexts


def _even_w_in(w):
    o = [0]
    for s in (2048, 512, 512, 1024, 64, 16, 2048, 3072, 32):
        o.append(o[-1] + s)
    q, k, v, qi, ki, wi, z, xbc, dt = [w[:, o[i]:o[i + 1]] for i in range(9)]
    pad = lambda t, n: jnp.concatenate([t, jnp.zeros((t.shape[0], n - t.shape[1]), t.dtype)], axis=1)
    cols = [q, z, xbc[:, :SSM_D], qi, xbc[:, SSM_D:], k, v, ki, ki, pad(wi, LANES), pad(dt, LANES)]
    return jnp.concatenate(cols, axis=1).astype(BF16)


def kernel(x_prompt, x_sample, cache_k, cache_v, cache_idx_k, state_ssm, state_ssm_conv, state_sconv, page_table,
           c_prompt, c_sample, ada_w, ada_b, ln_g, ln_b, even_w_in, even_w_out, ssm_conv_w, ssm_conv_b,
           ssm_dt_bias, ssm_a_log, ssm_d, ssm_norm_w, odd_w_in, odd_w_out, sconv_w, gmlp_ln_g, gmlp_ln_b,
           gmlp_ws, gmlp_bs, router_w, router_b, moe_w_gu, moe_b_gu, moe_w_down, moe_b_down):
    bp, seq, d = x_prompt.shape
    bs, dseq, _ = x_sample.shape
    depth = ada_w.shape[0]
    n_mod = 16
    c_all = jnp.zeros((n_mod, d), F32).at[:bp].set(c_prompt).at[bp:bp + bs].set(c_sample)
    m_all = adaln_all(c_all, ada_w, ada_b)
    mods = [[Mod(m_all[l, :bp], seq, False), Mod(m_all[l, bp:bp + bs], dseq, True)] for l in range(depth)]
    groups = [(bp, seq), (bs, dseq)]
    xs = [x_prompt.reshape(bp * seq, d), x_sample.reshape(bs * dseq, d)]
    hs = [modulate(xs[g], mods[0][g]) for g in range(2)]
    page_table = page_table.astype(I32)

    kp, vp, ikp, ks_, vs_, iks = [], [], [], [], [], []
    ssm_o = [[], []]
    ssmconv_o = [[], []]
    sconv_o = [[], []]
    gmv_o = [[], []]
    for layer in range(depth):
        j = layer // 2
        ya, yb = [None, None], [None, None]
        if layer % 2 == 0:
            w_in = _even_w_in(even_w_in[j])
            mamba_w = (ssm_conv_w[j], ssm_conv_b[j], ssm_dt_bias[j], ssm_a_log[j], ssm_d[j], ssm_norm_w[j])
            for g, (nb, sq) in enumerate(groups):
                proj = matmul(hs[g], w_in, EV_TN)
                p3 = proj.reshape(nb, sq, EV_N)
                k_new = p3[:, :, EV_K:EV_K + A_KV]
                v_new = p3[:, :, EV_V:EV_V + A_KV]
                ki_new = p3[:, :, EV_KK:EV_KK + IDX_DIM]
                xbc_tail = jnp.concatenate([p3[:, sq - (SSM_CONV - 1):, EV_XS:EV_XS + SSM_D],
                                            p3[:, sq - (SSM_CONV - 1):, EV_BC:EV_BC + BC_DIM]], axis=-1)
                if g == 0:
                    ya[g] = dsa_prompt(proj, nb, sq)
                    conv_prev = jnp.zeros((nb, SSM_CONV - 1, SSM_D + BC_DIM), F32)
                    h0 = jnp.zeros((nb, SSM_HEADS, SSM_HEAD_DIM, SSM_STATE), F32)
                    kp.append(k_new.reshape(nb, sq, ATT_KV_HEADS, HEAD_DIM))
                    vp.append(v_new.reshape(nb, sq, ATT_KV_HEADS, HEAD_DIM))
                    ikp.append(ki_new)
                else:
                    qs_rows = p3[:, :, EV_QI:EV_QI + IDX_HEADS * IDX_DIM].reshape(nb, sq * IDX_HEADS, IDX_DIM)
                    wcol = p3[:, :, EV_WI:EV_WI + IDX_HEADS].reshape(nb, sq * IDX_HEADS, 1)
                    scores = dsa_sample_scores(page_table, qs_rows, wcol, cache_idx_k, j)
                    padr = lambda t: jnp.concatenate(
                        [t, jnp.zeros((nb, LANES - sq, t.shape[2]), t.dtype)], axis=1)
                    att = dsa_sample_attend(page_table, p3[:, :, EV_Q:EV_Q + D_MODEL], qs_rows, wcol, padr(ki_new),
                                            scores, padr(k_new), padr(v_new), cache_k, cache_v, j)
                    ya[g] = att.reshape(nb * sq, D_MODEL)
                    conv_prev = state_ssm_conv[j]
                    h0 = state_ssm[j]
                    ks_.append(k_new.reshape(nb, sq, ATT_KV_HEADS, HEAD_DIM))
                    vs_.append(v_new.reshape(nb, sq, ATT_KV_HEADS, HEAD_DIM))
                    iks.append(ki_new)
                yb[g], h_last = ssd_mixer(proj, nb, sq, conv_prev, h0, *mamba_w)
                ssm_o[g].append(h_last)
                ssmconv_o[g].append(xbc_tail)
            w_out = even_w_out[j]
        else:
            w_in = odd_w_in[j].astype(BF16)
            for g, (nb, sq) in enumerate(groups):
                proj = matmul(hs[g], w_in, OD_TN)
                prev = jnp.zeros((nb, SC_WIDTH - 1, D_MODEL), F32) if g == 0 else state_sconv[j]
                ya[g], yb[g], sc_state, gm_v = odd_mixer(proj, nb, sq, prev, sconv_w[j], gmlp_ln_g[j], gmlp_ln_b[j],
                                                         gmlp_ws[j], gmlp_bs[j])
                sconv_o[g].append(sc_state)
                gmv_o[g].append(gm_v)
            w_out = odd_w_out[j]
        op_args = (ln_g[layer, 0], ln_b[layer, 0], router_w[layer], router_b[layer])
        x1s, hs_s, tes, tgs = outproj_ln_router(ya[1], yb[1], w_out, xs[1], mods[layer][1], *op_args)
        tail = jnp.zeros((OUTPROJ_TM, d), F32).at[:hs_s.shape[0]].set(hs_s)
        x1p, h_all, tep, tgp = outproj_ln_router(ya[0], yb[0], w_out, xs[0], mods[layer][0], *op_args, tail=tail)
        next_mods = mods[layer + 1] if layer + 1 < depth else None
        xs, hs = moe_sublayer([x1p, x1s], h_all, [tep, tes], [tgp, tgs], mods[layer], next_mods,
                              ln_g[layer, 1], ln_b[layer, 1], moe_w_gu, moe_b_gu, moe_w_down, moe_b_down, layer)
    return (xs[0].reshape(bp, seq, d), xs[1].reshape(bs, dseq, d),
            jnp.stack(kp), jnp.stack(vp), jnp.stack(ikp),
            jnp.stack(ks_), jnp.stack(vs_), jnp.stack(iks),
            jnp.stack(ssm_o[0]), jnp.stack(ssm_o[1]),
            jnp.stack(ssmconv_o[0]), jnp.stack(ssmconv_o[1]),
            jnp.stack(sconv_o[0]), jnp.stack(sconv_o[1]),
            jnp.stack(gmv_o[0]), jnp.stack(gmv_o[1]))
```
